```python
import jax, jax.numpy as jnp
from jax import lax
import numpy as np

D_MODEL = 4096
BATCH = 4
SEQ = 4096
DEPTH = 4
DEC_BATCH = 8
DEC_SEQ = 64
PAST_LEN = 1024

CHUNK = 64
HEAD_DIM = 128
D_SB = D_MODEL // 2
N_HEADS_SB = D_SB // HEAD_DIM
POOL_WINDOWS = (2, 4, 8, 16)
POOL_GROUPS = len(POOL_WINDOWS)
D_POOL = D_MODEL // 4
POOL_GROUP_DIM = D_POOL // POOL_GROUPS
POOL_HIST = max(POOL_WINDOWS) - 1
GMLP_GROUPS = 4
D_GMLP = D_MODEL // 4
GMLP_GROUP_DIM = D_GMLP // GMLP_GROUPS
GMLP_CHUNK = 128
D_IN = 3 * D_SB + D_POOL + 2 * D_GMLP
N_BRANCH = 3
D_FF = 4 * D_MODEL
QBLOCK = 128
EPS = 1e-6

kernel_name = 'hybrid_stickbreak_pool_gmlp_stream_step'


def rms_norm(x, g):
    xf = x.astype(jnp.float32)
    y = xf * lax.rsqrt(jnp.mean(xf * xf, axis=-1, keepdims=True) + EPS)
    return (y * g.astype(jnp.float32)).astype(x.dtype)


def modulate(h, shift, scale):
    return h * (1 + scale[:, None, :]) + shift[:, None, :]


def stick_breaking_block(q, k, v, q_pos, k_pos):
    z = jnp.einsum('bqhd,bkhd->bhqk', q, k).astype(jnp.float32) * (HEAD_DIM ** -0.5)
    mask = k_pos[None, :] < q_pos[:, None]
    log_1m_beta = jnp.where(mask, jax.nn.log_sigmoid(-z), 0.0)
    between = lax.cumsum(log_1m_beta, axis=3, reverse=True) - log_1m_beta
    w = jnp.where(mask, jnp.exp(jax.nn.log_sigmoid(z) + between), 0.0)
    return jnp.einsum('bhqk,bkhd->bqhd', w.astype(v.dtype), v)


def stick_breaking(q, k, v, offset):
    B, Tq, H, hd = q.shape
    Tk = k.shape[1]
    k_pos = jnp.arange(Tk)
    q_pos = offset + jnp.arange(Tq)
    qb = QBLOCK if Tq % QBLOCK == 0 else Tq
    nb = Tq // qb
    qs = q.reshape(B, nb, qb, H, hd).transpose(1, 0, 2, 3, 4)
    ps = q_pos.reshape(nb, qb)
    out = lax.map(lambda a: stick_breaking_block(a[0], k, v, a[1], k_pos), (qs, ps))
    return out.transpose(1, 0, 2, 3, 4).reshape(B, Tq, H * hd)


def multiscale_pool(p, hist, offset, w_pool, s_pool):
    B, L, _ = p.shape
    xcat = jnp.concatenate([hist, p], axis=1)
    xf = xcat.astype(jnp.float32)
    cs = jnp.concatenate([jnp.zeros((B, 1, D_POOL), jnp.float32), jnp.cumsum(xf, axis=1)], axis=1)
    pos = offset + jnp.arange(L)
    end = cs[:, POOL_HIST + 1:POOL_HIST + 1 + L]
    pf = p.astype(jnp.float32)
    outs = []
    for g, w in enumerate(POOL_WINDOWS):
        sl = slice(g * POOL_GROUP_DIM, (g + 1) * POOL_GROUP_DIM)
        start = cs[:, POOL_HIST + 1 - w:POOL_HIST + 1 - w + L, sl]
        cnt = jnp.minimum(w, pos + 1).astype(jnp.float32)[None, :, None]
        outs.append((end[..., sl] - start) / cnt - pf[..., sl])
    d = jnp.stack(outs, axis=2).astype(p.dtype)
    y = jnp.einsum('blgc,gcd->blgd', d, w_pool).reshape(B, L, D_POOL)
    return y * s_pool, xcat[:, -POOL_HIST:]


def spatial_gating(gm, w_sp, b_sp, g_v):
    B, L, _ = gm.shape
    gm = jax.nn.gelu(gm)
    u, v = gm[..., :D_GMLP], gm[..., D_GMLP:]
    v = rms_norm(v, g_v)
    lc = GMLP_CHUNK if L % GMLP_CHUNK == 0 else L
    n = L // lc
    wm = (w_sp * jnp.tril(jnp.ones((GMLP_CHUNK, GMLP_CHUNK), w_sp.dtype)))[:, :lc, :lc]
    vc = v.reshape(B, n, lc, GMLP_GROUPS, GMLP_GROUP_DIM)
    sv = jnp.einsum('gts,bnsgc->bntgc', wm, vc) + b_sp[:, :lc].T[None, None, :, :, None]
    return u * sv.reshape(B, L, D_GMLP), v


def trunk_layer(x, c, hist_k, hist_v, hist_pool, offset, prm):
    B, L, _ = x.shape
    mod = jnp.dot(jax.nn.silu(c), prm['w_ada']) + prm['b_ada']
    sh1, sc1, gt1, sh2, sc2, gt2 = jnp.split(mod, 6, axis=-1)
    h = modulate(rms_norm(x, prm['g_n1']), sh1, sc1)
    z = h @ prm['w_in']
    q, k, v, p, gm = jnp.split(z, [D_SB, 2 * D_SB, 3 * D_SB, 3 * D_SB + D_POOL], axis=-1)
    q = rms_norm(q.reshape(B, L, N_HEADS_SB, HEAD_DIM), prm['g_q'])
    k = rms_norm(k.reshape(B, L, N_HEADS_SB, HEAD_DIM), prm['g_k'])
    v = v.reshape(B, L, N_HEADS_SB, HEAD_DIM)
    if hist_k is None:
        kk, vv = k, v
    else:
        kk = jnp.concatenate([hist_k, k], axis=1)
        vv = jnp.concatenate([hist_v, v], axis=1)
    a = stick_breaking(q, kk, vv, offset)
    pb, new_pool = multiscale_pool(p, hist_pool, offset, prm['w_pool'], prm['s_pool'])
    cg, new_gv = spatial_gating(gm, prm['w_sp'], prm['b_sp'], prm['g_v'])
    gates = jax.nn.sigmoid(h @ prm['w_gate'] + prm['b_gate'])
    ga, gb, gc = jnp.split(gates, N_BRANCH, axis=-1)
    merged = ga * (a @ prm['w_br_a']) + gb * (pb @ prm['w_br_b']) + gc * (cg @ prm['w_br_c'])
    x = x + gt1[:, None, :] * (merged @ prm['w_out'])
    h2 = modulate(rms_norm(x, prm['g_n2']), sh2, sc2)
    f = jnp.square(jax.nn.relu(h2 @ prm['w_ff1'])) @ prm['w_ff2']
    x = x + gt2[:, None, :] * f
    return x, k, v, new_pool, new_gv


def setup_inputs(seed: int = 0) -> dict:
    key = jax.random.key(seed)
    keys = iter(jax.random.split(key, 40))

    def nrm(shape, s):
        return jax.random.normal(next(keys), shape, jnp.float32) * s

    def gain(shape, s=0.05):
        return 1.0 + nrm(shape, s)

    return {
        'x_prompt': nrm((BATCH, SEQ, D_MODEL), 1.0),
        'x_sample': nrm((DEC_BATCH, DEC_SEQ, D_MODEL), 1.0),
        'c_prompt': nrm((BATCH, D_MODEL), 1.0),
        'c_sample': nrm((DEC_BATCH, D_MODEL), 1.0),
        'cache_k': nrm((DEPTH, DEC_BATCH, PAST_LEN, N_HEADS_SB, HEAD_DIM), 1.0),
        'cache_v': nrm((DEPTH, DEC_BATCH, PAST_LEN, N_HEADS_SB, HEAD_DIM), 1.0),
        'state_pool': nrm((DEPTH, DEC_BATCH, POOL_HIST, D_POOL), 1.0),
        'w_ada': nrm((DEPTH, D_MODEL, 6 * D_MODEL), D_MODEL ** -0.5),
        'b_ada': nrm((DEPTH, 6 * D_MODEL), 0.01),
        'g_n1': gain((DEPTH, D_MODEL)),
        'w_in': nrm((DEPTH, D_MODEL, D_IN), D_MODEL ** -0.5),
        'g_q': gain((DEPTH, HEAD_DIM)),
        'g_k': gain((DEPTH, HEAD_DIM)),
        'w_pool': nrm((DEPTH, POOL_GROUPS, POOL_GROUP_DIM, POOL_GROUP_DIM), POOL_GROUP_DIM ** -0.5),
        's_pool': gain((DEPTH, D_POOL), 0.1),
        'w_sp': nrm((DEPTH, GMLP_GROUPS, GMLP_CHUNK, GMLP_CHUNK), GMLP_CHUNK ** -0.5),
        'b_sp': gain((DEPTH, GMLP_GROUPS, GMLP_CHUNK), 0.1),
        'g_v': gain((DEPTH, D_GMLP)),
        'w_br_a': nrm((DEPTH, D_SB, D_MODEL), D_SB ** -0.5),
        'w_br_b': nrm((DEPTH, D_POOL, D_MODEL), D_POOL ** -0.5),
        'w_br_c': nrm((DEPTH, D_GMLP, D_MODEL), D_GMLP ** -0.5),
        'w_gate': nrm((DEPTH, D_MODEL, N_BRANCH * D_MODEL), D_MODEL ** -0.5),
        'b_gate': nrm((DEPTH, N_BRANCH * D_MODEL), 0.01),
        'w_out': nrm((DEPTH, D_MODEL, D_MODEL), D_MODEL ** -0.5),
        'g_n2': gain((DEPTH, D_MODEL)),
        'w_ff1': nrm((DEPTH, D_MODEL, D_FF), D_MODEL ** -0.5),
        'w_ff2': nrm((DEPTH, D_FF, D_MODEL), D_FF ** -0.5),
    }


def reference(x_prompt, x_sample, c_prompt, c_sample, cache_k, cache_v, state_pool,
              w_ada, b_ada, g_n1, w_in, g_q, g_k, w_pool, s_pool, w_sp, b_sp, g_v,
              w_br_a, w_br_b, w_br_c, w_gate, b_gate, w_out, g_n2, w_ff1, w_ff2):
    past = cache_k.shape[2]
    yp, ys = x_prompt, x_sample
    kp_l, vp_l, pp_l, ks_l, vs_l, ps_l, gs_l = [], [], [], [], [], [], []
    for l in range(DEPTH):
        prm = {
            'w_ada': w_ada[l], 'b_ada': b_ada[l], 'g_n1': g_n1[l], 'w_in': w_in[l],
            'g_q': g_q[l], 'g_k': g_k[l], 'w_pool': w_pool[l], 's_pool': s_pool[l],
            'w_sp': w_sp[l], 'b_sp': b_sp[l], 'g_v': g_v[l],
            'w_br_a': w_br_a[l], 'w_br_b': w_br_b[l], 'w_br_c': w_br_c[l],
            'w_gate': w_gate[l], 'b_gate': b_gate[l], 'w_out': w_out[l],
            'g_n2': g_n2[l], 'w_ff1': w_ff1[l], 'w_ff2': w_ff2[l],
        }
        zero_hist = jnp.zeros((yp.shape[0], POOL_HIST, D_POOL), yp.dtype)
        yp, kp, vp, pp, _ = trunk_layer(yp, c_prompt, None, None, zero_hist, 0, prm)
        ys, ks, vs, ps, gs = trunk_layer(ys, c_sample, cache_k[l], cache_v[l], state_pool[l], past, prm)
        kp_l.append(kp); vp_l.append(vp); pp_l.append(pp)
        ks_l.append(ks); vs_l.append(vs); ps_l.append(ps); gs_l.append(gs)
    new_k_prompt = jnp.stack(kp_l)
    new_v_prompt = jnp.stack(vp_l)
    new_pool_prompt = jnp.stack(pp_l)
    new_k_sample = jnp.stack(ks_l)
    new_v_sample = jnp.stack(vs_l)
    new_pool_sample = jnp.stack(ps_l)
    new_gv_sample = jnp.stack(gs_l)
    return (yp, ys, new_k_prompt, new_v_prompt, new_pool_prompt,
            new_k_sample, new_v_sample, new_pool_sample, new_gv_sample)
```

```python
import functools

import jax
import jax.numpy as jnp
from jax import lax
from jax.experimental import pallas as pl
from jax.experimental.pallas import tpu as pltpu

_EPS = 1e-6
_HEAD_DIM = 128
_POOL_WINDOWS = (2, 4, 8, 16)
_POOL_HIST = max(_POOL_WINDOWS) - 1
_HALO = 16
_GMLP_GROUPS = 4
_GMLP_CHUNK = 128
_N_MOD = 6
_N_BRANCH = 3
_C_ROWS = 16

_VMEM_LIMIT_BYTES = 56 * 1024 * 1024
_BLOCK_M = 1024
_BLOCK_N = 1024
_BLOCK_K = 2048
_ATTN_BLOCK = 256
_ROW_TILE = 512

_F32 = jnp.float32
_BF16 = jnp.bfloat16


def _cparams(*sem):
    return pltpu.CompilerParams(dimension_semantics=sem, vmem_limit_bytes=_VMEM_LIMIT_BYTES)


def _dot(a, b):
    return jnp.dot(a, b, preferred_element_type=_F32)


def _sigmoid(x):
    return 1.0 / (1.0 + jnp.exp(-x))


def _ada_kernel(c_ref, w_ref, b_ref, o_ref):
    c = c_ref[...]
    s = (c * _sigmoid(c)).astype(_BF16)
    o_ref[...] = _dot(s, w_ref[...].astype(_BF16)) + b_ref[...]


def _ada(c_all, w_ada, b_ada):
    depth, d, n = w_ada.shape
    bn = 512
    return pl.pallas_call(
        _ada_kernel,
        grid=(depth, n // bn),
        in_specs=[
            pl.BlockSpec((_C_ROWS, d), lambda l, j: (0, 0)),
            pl.BlockSpec((None, d, bn), lambda l, j: (l, 0, j)),
            pl.BlockSpec((None, 1, bn), lambda l, j: (l, 0, j)),
        ],
        out_specs=pl.BlockSpec((None, _C_ROWS, bn), lambda l, j: (l, 0, j)),
        out_shape=jax.ShapeDtypeStruct((depth, _C_ROWS, n), _F32),
        compiler_params=_cparams("arbitrary", "arbitrary"),
        name="ada",
    )(c_all, w_ada, b_ada.reshape(depth, 1, n))


def _norm_mod_kernel(x_ref, g_ref, sc_ref, sh_ref, o_ref):
    x = x_ref[...]
    ms = jnp.mean(x * x, axis=-1, keepdims=True)
    y = x * lax.rsqrt(ms + _EPS) * g_ref[...]
    o_ref[...] = (y * (1.0 + sc_ref[...]) + sh_ref[...]).astype(o_ref.dtype)


def _norm_mod(x, g, l, scale, shift):
    b, seq, d = x.shape
    bt = min(_ROW_TILE, seq)
    depth = g.shape[0]
    return pl.pallas_call(
        _norm_mod_kernel,
        grid=(b, seq // bt),
        in_specs=[
            pl.BlockSpec((None, bt, d), lambda i, j: (i, j, 0)),
            pl.BlockSpec((None, 1, d), lambda i, j: (l, 0, 0)),
            pl.BlockSpec((None, 1, d), lambda i, j: (i, 0, 0)),
            pl.BlockSpec((None, 1, d), lambda i, j: (i, 0, 0)),
        ],
        out_specs=pl.BlockSpec((None, bt, d), lambda i, j: (i, j, 0)),
        out_shape=jax.ShapeDtypeStruct((b, seq, d), _BF16),
        compiler_params=_cparams("arbitrary", "arbitrary"),
        name="norm_mod",
    )(x, g.reshape(depth, 1, d), scale.reshape(b, 1, d), shift.reshape(b, 1, d))


def _mm_call(kern, x, w, l, col0, ncols, extra_in, extra_specs, out_shape, out_specs, name):
    m, k = x.shape
    bm = min(_BLOCK_M, m)
    bn = min(_BLOCK_N, ncols)
    cb = col0 // bn
    in_specs = [
        pl.BlockSpec((bm, k), lambda j, i: (i, 0)),
        pl.BlockSpec((None, k, bn), lambda j, i: (l, 0, j + cb)),
    ] + list(extra_specs(bm, bn))
    return pl.pallas_call(
        kern,
        grid=(ncols // bn, m // bm),
        in_specs=in_specs,
        out_specs=out_specs(bm, bn),
        out_shape=out_shape,
        compiler_params=_cparams("arbitrary", "arbitrary"),
        name=name,
    )(x, w, *extra_in)


def _tile_spec(bm, bn):
    return pl.BlockSpec((bm, bn), lambda j, i: (i, j))


def _head_norm_store(acc, g, out_refs):
    for h in range(acc.shape[1] // _HEAD_DIM):
        cols = slice(h * _HEAD_DIM, (h + 1) * _HEAD_DIM)
        a = acc[:, cols]
        ms = jnp.mean(a * a, axis=-1, keepdims=True)
        y = a * lax.rsqrt(ms + _EPS) * g
        for o_ref in out_refs:
            o_ref[:, cols] = y.astype(o_ref.dtype)


def _qk_kernel(x_ref, w_ref, g_ref, *out_refs):
    _head_norm_store(_dot(x_ref[...], w_ref[...]), g_ref[...], out_refs)


def _copy_kernel(x_ref, w_ref, *out_refs):
    acc = _dot(x_ref[...], w_ref[...])
    for o_ref in out_refs:
        o_ref[...] = acc.astype(o_ref.dtype)


def _gelu_kernel(x_ref, w_ref, o_ref):
    o_ref[...] = jax.nn.gelu(_dot(x_ref[...], w_ref[...])).astype(o_ref.dtype)


def _gelu_norm_kernel(x_ref, w_ref, g_ref, *out_refs):
    v = jax.nn.gelu(_dot(x_ref[...], w_ref[...]))
    ms = jnp.mean(v * v, axis=-1, keepdims=True)
    y = v * lax.rsqrt(ms + _EPS) * g_ref[...]
    for o_ref in out_refs:
        o_ref[...] = y.astype(o_ref.dtype)


def _gates_kernel(x_ref, w_ref, b_ref, o_ref):
    o_ref[...] = _sigmoid(_dot(x_ref[...], w_ref[...]) + b_ref[...]).astype(o_ref.dtype)


def _relu2_kernel(x_ref, w_ref, o_ref):
    r = jnp.maximum(_dot(x_ref[...], w_ref[...]), 0.0)
    o_ref[...] = (r * r).astype(o_ref.dtype)


def _gated_residual(x_ref, gt_ref, acc, o_ref, i, seq):
    bm = acc.shape[0]
    if seq >= bm:
        b = (i * bm) // seq
        o_ref[...] = x_ref[...] + gt_ref[pl.ds(b, 1), :] * acc
    else:
        per = bm // seq
        for s in range(per):
            rows = slice(s * seq, (s + 1) * seq)
            o_ref[rows, :] = x_ref[rows, :] + gt_ref[pl.ds(i * per + s, 1), :] * acc[rows, :]


def _out_kernel(m_ref, w_ref, x_ref, gt_ref, o_ref, *, seq):
    _gated_residual(x_ref, gt_ref, _dot(m_ref[...], w_ref[...]), o_ref, pl.program_id(1), seq)


def _merge_kernel(a_ref, pb_ref, cg_ref, wa_ref, wb_ref, wc_ref, ga_ref, gb_ref, gc_ref, o_ref):
    ya = _dot(a_ref[...], wa_ref[...])
    yb = _dot(pb_ref[...], wb_ref[...])
    yc = _dot(cg_ref[...], wc_ref[...])
    o = ga_ref[...].astype(_F32) * ya + gb_ref[...].astype(_F32) * yb + gc_ref[...].astype(_F32) * yc
    o_ref[...] = o.astype(o_ref.dtype)


def _merge(a, pb, cg, wa, wb, wc, gates, l):
    m = a.shape[0]
    d = wa.shape[2]
    bm = min(_BLOCK_M // 2, m)
    bn = min(_BLOCK_N, d)
    nb = d // bn

    def act(x):
        return pl.BlockSpec((bm, x.shape[1]), lambda j, i: (i, 0))

    def wt(w):
        return pl.BlockSpec((None, w.shape[1], bn), lambda j, i: (l, 0, j))

    def gate(branch):
        return pl.BlockSpec((bm, bn), lambda j, i: (i, j + branch * nb))

    return pl.pallas_call(
        _merge_kernel,
        grid=(nb, m // bm),
        in_specs=[act(a), act(pb), act(cg), wt(wa), wt(wb), wt(wc), gate(0), gate(1), gate(2)],
        out_specs=_tile_spec(bm, bn),
        out_shape=jax.ShapeDtypeStruct((m, d), _BF16),
        compiler_params=_cparams("arbitrary", "arbitrary"),
        name="merge",
    )(a, pb, cg, wa, wb, wc, gates, gates, gates)


def _ff2_kernel(f_ref, w_ref, x_ref, gt_ref, o_ref, *, seq, nk):
    k = pl.program_id(2)
    part = _dot(f_ref[...], w_ref[...])
    if nk == 1:
        _gated_residual(x_ref, gt_ref, part, o_ref, pl.program_id(1), seq)
        return

    @pl.when(k == 0)
    def _():
        o_ref[...] = part

    @pl.when(jnp.logical_and(k > 0, k < nk - 1))
    def _():
        o_ref[...] += part

    @pl.when(k == nk - 1)
    def _():
        _gated_residual(x_ref, gt_ref, o_ref[...] + part, o_ref, pl.program_id(1), seq)


def _ff2(f, w, l, x, gt, seq):
    m, kdim = f.shape
    d = w.shape[2]
    bm = min(_BLOCK_M, m)
    bn = min(_BLOCK_N, d)
    bk = min(_BLOCK_K, kdim)
    nk = kdim // bk
    nbatch = gt.shape[0]
    return pl.pallas_call(
        functools.partial(_ff2_kernel, seq=seq, nk=nk),
        grid=(d // bn, m // bm, nk),
        in_specs=[
            pl.BlockSpec((bm, bk), lambda j, i, k: (i, k)),
            pl.BlockSpec((None, bk, bn), lambda j, i, k: (l, k, j)),
            pl.BlockSpec((bm, bn), lambda j, i, k: (i, j)),
            pl.BlockSpec((nbatch, bn), lambda j, i, k: (0, j)),
        ],
        out_specs=pl.BlockSpec((bm, bn), lambda j, i, k: (i, j)),
        out_shape=jax.ShapeDtypeStruct((m, d), _F32),
        compiler_params=_cparams("arbitrary", "arbitrary", "arbitrary"),
        name="ff2",
    )(f, w, x, gt)


def _softplus(s):
    return jnp.maximum(s, 0.0) + jnp.log(1.0 + jnp.exp(-jnp.abs(s)))


def _suffix_sum(lb, tri):
    hi = lb.astype(_BF16)
    lo = (lb - hi.astype(_F32)).astype(_BF16)
    return _dot(hi, tri) + _dot(lo, tri)


def _attn_kernel(q_ref, k_ref, v_ref, tri_ref, o_ref, *, bq, bk, off):
    scale = _HEAD_DIM ** -0.5
    q = q_ref[...]
    row0 = off + pl.program_id(2) * bq

    def scores(kb):
        return lax.dot_general(q, kb, (((1,), (1,)), ((), ())), preferred_element_type=_F32) * scale

    start = pl.multiple_of(row0, bq)
    s = scores(k_ref[pl.ds(start, bq), :])
    mask = lax.broadcasted_iota(jnp.int32, (bq, bq), 1) < lax.broadcasted_iota(jnp.int32, (bq, bq), 0)
    sp = _softplus(s)
    lb = jnp.where(mask, -sp, 0.0)
    w = jnp.where(mask, jnp.exp(s - sp + _suffix_sum(lb, tri_ref[:bq, :bq])), 0.0)
    acc = _dot(w.astype(_BF16), v_ref[pl.ds(start, bq), :])
    run = jnp.sum(lb, axis=-1, keepdims=True)

    def body(step, carry):
        acc, run = carry
        kstart = pl.multiple_of(row0 - (step + 1) * bk, bk)
        s = scores(k_ref[pl.ds(kstart, bk), :])
        sp = _softplus(s)
        lb = -sp
        w = jnp.exp(s - sp + _suffix_sum(lb, tri_ref[...]) + run)
        acc = acc + _dot(w.astype(_BF16), v_ref[pl.ds(kstart, bk), :])
        return acc, run + jnp.sum(lb, axis=-1, keepdims=True)

    acc, _ = lax.fori_loop(0, row0 // bk, body, (acc, run))
    o_ref[...] = acc.astype(o_ref.dtype)


def _attention(q, k, v, off):
    b, tq, dsb = q.shape
    tk = k.shape[1]
    nh = dsb // _HEAD_DIM
    bk = _ATTN_BLOCK
    bq = min(_ATTN_BLOCK, tq)
    assert off % bk == 0 and (bq == bk or tq == bq)
    idx = lax.broadcasted_iota(jnp.int32, (bk, bk), 0) > lax.broadcasted_iota(jnp.int32, (bk, bk), 1)
    tri = idx.astype(_BF16)
    return pl.pallas_call(
        functools.partial(_attn_kernel, bq=bq, bk=bk, off=off),
        grid=(b, nh, tq // bq),
        in_specs=[
            pl.BlockSpec((None, bq, _HEAD_DIM), lambda i, h, j: (i, j, h)),
            pl.BlockSpec((None, tk, _HEAD_DIM), lambda i, h, j: (i, 0, h)),
            pl.BlockSpec((None, tk, _HEAD_DIM), lambda i, h, j: (i, 0, h)),
            pl.BlockSpec((bk, bk), lambda i, h, j: (0, 0)),
        ],
        out_specs=pl.BlockSpec((None, bq, _HEAD_DIM), lambda i, h, j: (i, j, h)),
        out_shape=jax.ShapeDtypeStruct((b, tq, dsb), _BF16),
        compiler_params=_cparams("arbitrary", "arbitrary", "arbitrary"),
        name="attention",
    )(q, k, v, tri)


def _pool_kernel(p_ref, prev_ref, hist_ref, wp_ref, sp_ref, o_ref, xcat_ref, *, bt, off, gd):
    j = pl.program_id(1)
    xcat_ref[0:_HALO, :] = jnp.where(j == 0, hist_ref[...], prev_ref[...])
    xcat_ref[_HALO:, :] = p_ref[...]
    pos = off + j * bt + lax.broadcasted_iota(jnp.int32, (bt, 1), 0)
    for g, win in enumerate(_POOL_WINDOWS):
        cols = slice(g * gd, (g + 1) * gd)
        tot = xcat_ref[_HALO:_HALO + bt, cols]
        for back in range(1, win):
            tot = tot + xcat_ref[_HALO - back:_HALO - back + bt, cols]
        cnt = jnp.minimum(win, pos + 1).astype(_F32)
        diff = (tot / cnt - p_ref[:, cols]).astype(_BF16)
        y = _dot(diff, wp_ref[g].astype(_BF16))
        o_ref[:, cols] = (y * sp_ref[:, cols]).astype(o_ref.dtype)


def _pool(p, hist, off, w_pool, s_pool, l):
    b, seq, dp = p.shape
    depth, ng, gd, _ = w_pool.shape
    bt = min(_ROW_TILE, seq)
    per = bt // _HALO
    return pl.pallas_call(
        functools.partial(_pool_kernel, bt=bt, off=off, gd=gd),
        grid=(b, seq // bt),
        in_specs=[
            pl.BlockSpec((None, bt, dp), lambda i, j: (i, j, 0)),
            pl.BlockSpec((None, _HALO, dp), lambda i, j: (i, jnp.maximum(j * per - 1, 0), 0)),
            pl.BlockSpec((None, _HALO, dp), lambda i, j: (i, 0, 0)),
            pl.BlockSpec((None, ng, gd, gd), lambda i, j: (l, 0, 0, 0)),
            pl.BlockSpec((None, 1, dp), lambda i, j: (l, 0, 0)),
        ],
        out_specs=pl.BlockSpec((None, bt, dp), lambda i, j: (i, j, 0)),
        out_shape=jax.ShapeDtypeStruct((b, seq, dp), _BF16),
        scratch_shapes=[pltpu.VMEM((bt + _HALO, dp), _F32)],
        compiler_params=_cparams("arbitrary", "arbitrary"),
        name="pool",
    )(p, p, hist, w_pool, s_pool.reshape(depth, 1, dp))


def _spatial_kernel(u_ref, vn_ref, wsp_ref, bsp_ref, o_ref, *, bt, lc, gd):
    lower = lax.broadcasted_iota(jnp.int32, (lc, lc), 0) >= lax.broadcasted_iota(jnp.int32, (lc, lc), 1)
    for g in range(_GMLP_GROUPS):
        cols = slice(g * gd, (g + 1) * gd)
        wm = jnp.where(lower, wsp_ref[g, :lc, :lc], 0.0).astype(_BF16)
        bias = bsp_ref[:lc, g:g + 1]
        for c in range(bt // lc):
            rows = slice(c * lc, (c + 1) * lc)
            sv = _dot(wm, vn_ref[rows, cols]) + bias
            o_ref[rows, cols] = (u_ref[rows, cols].astype(_F32) * sv).astype(o_ref.dtype)


def _spatial(u, vn, w_sp, b_sp_t, l):
    b, seq, dg = u.shape
    lc = _GMLP_CHUNK if seq % _GMLP_CHUNK == 0 else seq
    bt = min(_ROW_TILE, seq)
    gd = dg // _GMLP_GROUPS
    return pl.pallas_call(
        functools.partial(_spatial_kernel, bt=bt, lc=lc, gd=gd),
        grid=(b, seq // bt),
        in_specs=[
            pl.BlockSpec((None, bt, dg), lambda i, j: (i, j, 0)),
            pl.BlockSpec((None, bt, dg), lambda i, j: (i, j, 0)),
            pl.BlockSpec((None, _GMLP_GROUPS, _GMLP_CHUNK, _GMLP_CHUNK), lambda i, j: (l, 0, 0, 0)),
            pl.BlockSpec((None, _GMLP_CHUNK, _GMLP_GROUPS), lambda i, j: (l, 0, 0)),
        ],
        out_specs=pl.BlockSpec((None, bt, dg), lambda i, j: (i, j, 0)),
        out_shape=jax.ShapeDtypeStruct((b, seq, dg), _BF16),
        compiler_params=_cparams("arbitrary", "arbitrary"),
        name="spatial",
    )(u, vn, w_sp, b_sp_t)


def _trunk_layer(x, mod, k_past, v_past, hist, off, l, prm, want_gv):
    b, seq, d = x.shape
    m = b * seq
    dsb = d // 2
    dp = d // 4
    dg = d // 4
    sh1, sc1, gt1, sh2, sc2, gt2 = [mod[:, i * d:(i + 1) * d] for i in range(_N_MOD)]

    def sds(n, dt):
        return jax.ShapeDtypeStruct((m, n), dt)

    def no_extra(bm, bn):
        return []

    def head_gain(bm, bn):
        return [pl.BlockSpec((None, 1, _HEAD_DIM), lambda j, i: (l, 0, 0))]

    def two(bm, bn):
        return [_tile_spec(bm, bn), _tile_spec(bm, bn)]

    h = _norm_mod(x, prm["g_n1"], l, sc1, sh1).reshape(m, d)
    w_in = prm["w_in"]
    depth = w_in.shape[0]
    q = _mm_call(_qk_kernel, h, w_in, l, 0, dsb, [prm["g_q"].reshape(depth, 1, _HEAD_DIM)], head_gain,
                 sds(dsb, _BF16), _tile_spec, "proj_q")
    k32, k16 = _mm_call(_qk_kernel, h, w_in, l, dsb, dsb, [prm["g_k"].reshape(depth, 1, _HEAD_DIM)], head_gain,
                        [sds(dsb, _F32), sds(dsb, _BF16)], two, "proj_k")
    v32, v16 = _mm_call(_copy_kernel, h, w_in, l, 2 * dsb, dsb, [], no_extra,
                        [sds(dsb, _F32), sds(dsb, _BF16)], two, "proj_v")
    p = _mm_call(_copy_kernel, h, w_in, l, 3 * dsb, dp, [], no_extra, sds(dp, _F32), _tile_spec, "proj_p")
    u = _mm_call(_gelu_kernel, h, w_in, l, 3 * dsb + dp, dg, [], no_extra, sds(dg, _BF16), _tile_spec, "proj_u")

    def gv_gain(bm, bn):
        return [pl.BlockSpec((None, 1, dg), lambda j, i: (l, 0, 0))]

    gv_in = [prm["g_v"].reshape(depth, 1, dg)]
    if want_gv:
        gv32, vn = _mm_call(_gelu_norm_kernel, h, w_in, l, 3 * dsb + dp + dg, dg, gv_in, gv_gain,
                            [sds(dg, _F32), sds(dg, _BF16)], two, "proj_gv")
    else:
        gv32 = None
        vn = _mm_call(_gelu_norm_kernel, h, w_in, l, 3 * dsb + dp + dg, dg, gv_in, gv_gain,
                      sds(dg, _BF16), _tile_spec, "proj_gv")

    def gate_bias(bm, bn):
        return [pl.BlockSpec((None, 1, bn), lambda j, i: (l, 0, j))]

    gates = _mm_call(_gates_kernel, h, prm["w_gate"], l, 0, _N_BRANCH * d,
                     [prm["b_gate"].reshape(depth, 1, _N_BRANCH * d)], gate_bias,
                     sds(_N_BRANCH * d, _BF16), _tile_spec, "gates")

    q3 = q.reshape(b, seq, dsb)
    k3 = k16.reshape(b, seq, dsb)
    v3 = v16.reshape(b, seq, dsb)
    if k_past is not None:
        k3 = jnp.concatenate([k_past.reshape(b, off, dsb).astype(_BF16), k3], axis=1)
        v3 = jnp.concatenate([v_past.reshape(b, off, dsb).astype(_BF16), v3], axis=1)
    a = _attention(q3, k3, v3, off).reshape(m, dsb)

    p3 = p.reshape(b, seq, dp)
    pb = _pool(p3, hist, off, prm["w_pool"], prm["s_pool"], l).reshape(m, dp)
    cg = _spatial(u.reshape(b, seq, dg), vn.reshape(b, seq, dg), prm["w_sp"], prm["b_sp_t"], l).reshape(m, dg)

    merged = _merge(a, pb, cg, prm["w_br_a"], prm["w_br_b"], prm["w_br_c"], gates, l)

    def resid(gt):
        def specs(bm, bn):
            return [_tile_spec(bm, bn), pl.BlockSpec((b, bn), lambda j, i: (0, j))]
        return specs

    x1 = _mm_call(functools.partial(_out_kernel, seq=seq), merged, prm["w_out"], l, 0, d,
                  [x.reshape(m, d), gt1], resid(gt1), sds(d, _F32), _tile_spec, "out_proj")
    h2 = _norm_mod(x1.reshape(b, seq, d), prm["g_n2"], l, sc2, sh2).reshape(m, d)
    f = _mm_call(_relu2_kernel, h2, prm["w_ff1"], l, 0, prm["w_ff1"].shape[2], [], no_extra,
                 sds(prm["w_ff1"].shape[2], _BF16), _tile_spec, "ff1")
    x2 = _ff2(f, prm["w_ff2"], l, x1, gt2, seq).reshape(b, seq, d)

    nh = dsb // _HEAD_DIM
    new_k = k32.reshape(b, seq, nh, _HEAD_DIM)
    new_v = v32.reshape(b, seq, nh, _HEAD_DIM)
    new_pool = p3[:, seq - _POOL_HIST:, :]
    new_gv = gv32.reshape(b, seq, dg) if want_gv else None
    return x2, new_k, new_v, new_pool, new_gv


def kernel(x_prompt, x_sample, c_prompt, c_sample, cache_k, cache_v, state_pool, w_ada, b_ada, g_n1, w_in, g_q, g_k, w_pool, s_pool, w_sp, b_sp, g_v, w_br_a, w_br_b, w_br_c, w_gate, b_gate, w_out, g_n2, w_ff1, w_ff2):
    depth = w_in.shape[0]
    nb_p = x_prompt.shape[0]
    nb_s = x_sample.shape[0]
    past = cache_k.shape[2]
    dp = state_pool.shape[-1]
    assert nb_p + nb_s <= _C_ROWS

    prm = {
        "g_n1": g_n1, "g_q": g_q, "g_k": g_k, "g_v": g_v, "g_n2": g_n2,
        "w_pool": w_pool, "s_pool": s_pool, "w_sp": w_sp, "b_sp_t": jnp.swapaxes(b_sp, 1, 2),
        "b_gate": b_gate,
        "w_in": w_in.astype(_BF16), "w_gate": w_gate.astype(_BF16),
        "w_br_a": w_br_a.astype(_BF16), "w_br_b": w_br_b.astype(_BF16), "w_br_c": w_br_c.astype(_BF16),
        "w_out": w_out.astype(_BF16), "w_ff1": w_ff1.astype(_BF16), "w_ff2": w_ff2.astype(_BF16),
    }

    c_all = jnp.concatenate(
        [c_prompt, c_sample, jnp.zeros((_C_ROWS - nb_p - nb_s, c_prompt.shape[1]), _F32)], axis=0)
    mod = _ada(c_all, w_ada, b_ada)

    hist_p = jnp.zeros((nb_p, _HALO, dp), _F32)
    hist_s = jnp.pad(state_pool, ((0, 0), (0, 0), (_HALO - _POOL_HIST, 0), (0, 0)))

    yp, ys = x_prompt, x_sample
    outs = [[] for _ in range(7)]
    for l in range(depth):
        yp, kp, vp, pp, _ = _trunk_layer(yp, mod[l, :nb_p], None, None, hist_p, 0, l, prm, False)
        ys, ks, vs, ps, gs = _trunk_layer(ys, mod[l, nb_p:nb_p + nb_s], cache_k[l], cache_v[l], hist_s[l],
                                          past, l, prm, True)
        for lst, val in zip(outs, (kp, vp, pp, ks, vs, ps, gs)):
            lst.append(val)
    return (yp, ys) + tuple(jnp.stack(o) for o in outs)
```

```python
import functools

import jax
import jax.numpy as jnp
from jax import lax
from jax.experimental import pallas as pl
from jax.experimental.pallas import tpu as pltpu

_EPS = 1e-6
_HEAD_DIM = 128
_POOL_WINDOWS = (2, 4, 8, 16)
_POOL_HIST = max(_POOL_WINDOWS) - 1
_HALO = 16
_GMLP_GROUPS = 4
_GMLP_CHUNK = 128
_N_MOD = 6
_N_BRANCH = 3
_C_ROWS = 16

_VMEM_LIMIT_BYTES = 56 * 1024 * 1024
_BLOCK_M = 1024
_BLOCK_N = 1024
_BLOCK_K = 2048
_ATTN_BLOCK = 512
_ATTN_CHUNK = 256
_ATTN_HEADS = 4
_LOG2E = 1.4426950408889634
_ROW_TILE = 512

_F32 = jnp.float32
_BF16 = jnp.bfloat16


def _cparams(*sem):
    return pltpu.CompilerParams(dimension_semantics=sem, vmem_limit_bytes=_VMEM_LIMIT_BYTES)


def _dot(a, b):
    return jnp.dot(a, b, preferred_element_type=_F32)


def _sigmoid(x):
    return 1.0 / (1.0 + jnp.exp(-x))


def _ada_kernel(c_ref, w_ref, b_ref, o_ref):
    c = c_ref[...]
    s = (c * _sigmoid(c)).astype(_BF16)
    o_ref[...] = _dot(s, w_ref[...].astype(_BF16)) + b_ref[...]


def _ada(c_all, w_ada, b_ada):
    depth, d, n = w_ada.shape
    bn = 512
    return pl.pallas_call(
        _ada_kernel,
        grid=(depth, n // bn),
        in_specs=[
            pl.BlockSpec((_C_ROWS, d), lambda l, j: (0, 0)),
            pl.BlockSpec((None, d, bn), lambda l, j: (l, 0, j)),
            pl.BlockSpec((None, 1, bn), lambda l, j: (l, 0, j)),
        ],
        out_specs=pl.BlockSpec((None, _C_ROWS, bn), lambda l, j: (l, 0, j)),
        out_shape=jax.ShapeDtypeStruct((depth, _C_ROWS, n), _F32),
        compiler_params=_cparams("arbitrary", "arbitrary"),
        name="ada",
    )(c_all, w_ada, b_ada.reshape(depth, 1, n))


def _norm_mod_kernel(x_ref, g_ref, sc_ref, sh_ref, o_ref):
    x = x_ref[...]
    ms = jnp.mean(x * x, axis=-1, keepdims=True)
    y = x * lax.rsqrt(ms + _EPS) * g_ref[...]
    o_ref[...] = (y * (1.0 + sc_ref[...]) + sh_ref[...]).astype(o_ref.dtype)


def _norm_mod(x, g, l, scale, shift):
    b, seq, d = x.shape
    bt = min(_ROW_TILE, seq)
    depth = g.shape[0]
    return pl.pallas_call(
        _norm_mod_kernel,
        grid=(b, seq // bt),
        in_specs=[
            pl.BlockSpec((None, bt, d), lambda i, j: (i, j, 0)),
            pl.BlockSpec((None, 1, d), lambda i, j: (l, 0, 0)),
            pl.BlockSpec((None, 1, d), lambda i, j: (i, 0, 0)),
            pl.BlockSpec((None, 1, d), lambda i, j: (i, 0, 0)),
        ],
        out_specs=pl.BlockSpec((None, bt, d), lambda i, j: (i, j, 0)),
        out_shape=jax.ShapeDtypeStruct((b, seq, d), _BF16),
        compiler_params=_cparams("arbitrary", "arbitrary"),
        name="norm_mod",
    )(x, g.reshape(depth, 1, d), scale.reshape(b, 1, d), shift.reshape(b, 1, d))


def _mm_call(kern, x, w, l, col0, ncols, extra_in, extra_specs, out_shape, out_specs, name):
    m, k = x.shape
    bm = min(_BLOCK_M, m)
    bn = min(_BLOCK_N, ncols)
    cb = col0 // bn
    in_specs = [
        pl.BlockSpec((bm, k), lambda j, i: (i, 0)),
        pl.BlockSpec((None, k, bn), lambda j, i: (l, 0, j + cb)),
    ] + list(extra_specs(bm, bn))
    return pl.pallas_call(
        kern,
        grid=(ncols // bn, m // bm),
        in_specs=in_specs,
        out_specs=out_specs(bm, bn),
        out_shape=out_shape,
        compiler_params=_cparams("arbitrary", "arbitrary"),
        name=name,
    )(x, w, *extra_in)


def _tile_spec(bm, bn):
    return pl.BlockSpec((bm, bn), lambda j, i: (i, j))


def _head_norm_store(acc, g, out_refs):
    for h in range(acc.shape[1] // _HEAD_DIM):
        cols = slice(h * _HEAD_DIM, (h + 1) * _HEAD_DIM)
        a = acc[:, cols]
        ms = jnp.mean(a * a, axis=-1, keepdims=True)
        y = a * lax.rsqrt(ms + _EPS) * g
        for o_ref in out_refs:
            o_ref[:, cols] = y.astype(o_ref.dtype)


def _qk_kernel(x_ref, w_ref, g_ref, *out_refs):
    _head_norm_store(_dot(x_ref[...], w_ref[...]), g_ref[...], out_refs)


def _copy_kernel(x_ref, w_ref, *out_refs):
    acc = _dot(x_ref[...], w_ref[...])
    for o_ref in out_refs:
        o_ref[...] = acc.astype(o_ref.dtype)


def _gelu_kernel(x_ref, w_ref, o_ref):
    o_ref[...] = jax.nn.gelu(_dot(x_ref[...], w_ref[...])).astype(o_ref.dtype)


def _gelu_norm_kernel(x_ref, w_ref, g_ref, *out_refs):
    v = jax.nn.gelu(_dot(x_ref[...], w_ref[...]))
    ms = jnp.mean(v * v, axis=-1, keepdims=True)
    y = v * lax.rsqrt(ms + _EPS) * g_ref[...]
    for o_ref in out_refs:
        o_ref[...] = y.astype(o_ref.dtype)


def _gates_kernel(x_ref, w_ref, b_ref, o_ref):
    o_ref[...] = _sigmoid(_dot(x_ref[...], w_ref[...]) + b_ref[...]).astype(o_ref.dtype)


def _relu2_kernel(x_ref, w_ref, o_ref):
    r = jnp.maximum(_dot(x_ref[...], w_ref[...]), 0.0)
    o_ref[...] = (r * r).astype(o_ref.dtype)


def _gated_residual(x_ref, gt_ref, acc, o_ref, i, seq):
    bm = acc.shape[0]
    if seq >= bm:
        b = (i * bm) // seq
        o_ref[...] = x_ref[...] + gt_ref[pl.ds(b, 1), :] * acc
    else:
        per = bm // seq
        for s in range(per):
            rows = slice(s * seq, (s + 1) * seq)
            o_ref[rows, :] = x_ref[rows, :] + gt_ref[pl.ds(i * per + s, 1), :] * acc[rows, :]


def _out_kernel(m_ref, w_ref, x_ref, gt_ref, o_ref, *, seq):
    _gated_residual(x_ref, gt_ref, _dot(m_ref[...], w_ref[...]), o_ref, pl.program_id(1), seq)


def _merge_kernel(a_ref, pb_ref, cg_ref, wa_ref, wb_ref, wc_ref, ga_ref, gb_ref, gc_ref, o_ref):
    ya = _dot(a_ref[...], wa_ref[...])
    yb = _dot(pb_ref[...], wb_ref[...])
    yc = _dot(cg_ref[...], wc_ref[...])
    o = ga_ref[...].astype(_F32) * ya + gb_ref[...].astype(_F32) * yb + gc_ref[...].astype(_F32) * yc
    o_ref[...] = o.astype(o_ref.dtype)


def _merge(a, pb, cg, wa, wb, wc, gates, l):
    m = a.shape[0]
    d = wa.shape[2]
    bm = min(_BLOCK_M // 2, m)
    bn = min(_BLOCK_N, d)
    nb = d // bn

    def act(x):
        return pl.BlockSpec((bm, x.shape[1]), lambda j, i: (i, 0))

    def wt(w):
        return pl.BlockSpec((None, w.shape[1], bn), lambda j, i: (l, 0, j))

    def gate(branch):
        return pl.BlockSpec((bm, bn), lambda j, i: (i, j + branch * nb))

    return pl.pallas_call(
        _merge_kernel,
        grid=(nb, m // bm),
        in_specs=[act(a), act(pb), act(cg), wt(wa), wt(wb), wt(wc), gate(0), gate(1), gate(2)],
        out_specs=_tile_spec(bm, bn),
        out_shape=jax.ShapeDtypeStruct((m, d), _BF16),
        compiler_params=_cparams("arbitrary", "arbitrary"),
        name="merge",
    )(a, pb, cg, wa, wb, wc, gates, gates, gates)


def _ff2_kernel(f_ref, w_ref, x_ref, gt_ref, o_ref, *, seq, nk):
    k = pl.program_id(2)
    part = _dot(f_ref[...], w_ref[...])
    if nk == 1:
        _gated_residual(x_ref, gt_ref, part, o_ref, pl.program_id(1), seq)
        return

    @pl.when(k == 0)
    def _():
        o_ref[...] = part

    @pl.when(jnp.logical_and(k > 0, k < nk - 1))
    def _():
        o_ref[...] += part

    @pl.when(k == nk - 1)
    def _():
        _gated_residual(x_ref, gt_ref, o_ref[...] + part, o_ref, pl.program_id(1), seq)


def _ff2(f, w, l, x, gt, seq):
    m, kdim = f.shape
    d = w.shape[2]
    bm = min(_BLOCK_M, m)
    bn = min(_BLOCK_N, d)
    bk = min(_BLOCK_K, kdim)
    nk = kdim // bk
    nbatch = gt.shape[0]
    return pl.pallas_call(
        functools.partial(_ff2_kernel, seq=seq, nk=nk),
        grid=(d // bn, m // bm, nk),
        in_specs=[
            pl.BlockSpec((bm, bk), lambda j, i, k: (i, k)),
            pl.BlockSpec((None, bk, bn), lambda j, i, k: (l, k, j)),
            pl.BlockSpec((bm, bn), lambda j, i, k: (i, j)),
            pl.BlockSpec((nbatch, bn), lambda j, i, k: (0, j)),
        ],
        out_specs=pl.BlockSpec((bm, bn), lambda j, i, k: (i, j)),
        out_shape=jax.ShapeDtypeStruct((m, d), _F32),
        compiler_params=_cparams("arbitrary", "arbitrary", "arbitrary"),
        name="ff2",
    )(f, w, x, gt)


def _attn_blocks(q_ref, k_ref, v_ref, tri, cols, rows, runs, mask):
    scores = [lax.dot_general(q_ref[:, c], k_ref[rows, c], (((1,), (1,)), ((), ())),
                              preferred_element_type=_F32) for c in cols]
    args, sps, sums = [], [], []
    for s in scores:
        s = s * (_HEAD_DIM ** -0.5 * _LOG2E)
        neg_abs = pltpu.bitcast(pltpu.bitcast(s, jnp.uint32) | jnp.uint32(0x80000000), _F32)
        sp = jnp.maximum(s, 0.0) + jnp.log(1.0 + jnp.exp2(neg_abs)) * _LOG2E
        args.append(s - sp)
        if mask is not None:
            sp = jnp.where(mask, sp, 0.0)
        sps.append(sp)
    n = sps[0].shape[1]
    cw = tri.shape[0]
    chunks = [slice(c * cw, (c + 1) * cw) for c in range(n // cw)]
    sufs, offsets = [], []
    for h, sp in enumerate(sps):
        sufs.append([_dot(sp[:, c].astype(_BF16), tri) for c in chunks])
        tail = None
        offs = [None] * len(chunks)
        for i in reversed(range(len(chunks))):
            if runs is None:
                offs[i] = tail
            else:
                offs[i] = runs[h] if tail is None else runs[h] + tail
            rs = jnp.sum(sp[:, chunks[i]], axis=-1, keepdims=True)
            tail = rs if tail is None else tail + rs
        offsets.append(offs)
        sums.append(tail)
    ws = []
    for arg, suf, offs in zip(args, sufs, offsets):
        parts = []
        for c, sf, of in zip(chunks, suf, offs):
            a = arg[:, c] - sf
            parts.append(jnp.exp2(a if of is None else a - of))
        w = parts[0] if len(parts) == 1 else jnp.concatenate(parts, axis=1)
        if mask is not None:
            w = jnp.where(mask, w, 0.0)
        ws.append(w.astype(_BF16))
    pvs = [_dot(w, v_ref[rows, c]) for w, c in zip(ws, cols)]
    return pvs, sums


def _attn_kernel(q_ref, k_ref, v_ref, tri_ref, o_ref, *, bq, bk, off, heads):
    row0 = off + pl.program_id(2) * bq
    start = pl.multiple_of(row0, bq)
    mask = lax.broadcasted_iota(jnp.int32, (bq, bq), 1) < lax.broadcasted_iota(jnp.int32, (bq, bq), 0)
    cols = [slice(h * _HEAD_DIM, (h + 1) * _HEAD_DIM) for h in range(heads)]

    dw = min(bq, tri_ref.shape[0])
    accs, runs = _attn_blocks(q_ref, k_ref, v_ref, tri_ref[:dw, :dw], cols, pl.ds(start, bq), None, mask)

    def body(step, carry):
        accs, runs = carry
        kstart = pl.multiple_of(row0 - (step + 1) * bk, bk)
        pvs, sums = _attn_blocks(q_ref, k_ref, v_ref, tri_ref[...], cols, pl.ds(kstart, bk), runs, None)
        return (tuple(a + p for a, p in zip(accs, pvs)), tuple(r + t for r, t in zip(runs, sums)))

    accs, _ = lax.fori_loop(0, row0 // bk, body, (tuple(accs), tuple(runs)))
    for c, acc in zip(cols, accs):
        o_ref[:, c] = acc.astype(o_ref.dtype)


def _attention(q, k, v, off):
    b, tq, dsb = q.shape
    tk = k.shape[1]
    nh = dsb // _HEAD_DIM
    bk = _ATTN_BLOCK
    bq = min(_ATTN_BLOCK, tq)
    cw = min(_ATTN_CHUNK, bk)
    assert off % bk == 0 and (bq == bk or tq == bq) and bk % cw == 0 and (bq % cw == 0 or bq < cw)
    idx = lax.broadcasted_iota(jnp.int32, (cw, cw), 0) > lax.broadcasted_iota(jnp.int32, (cw, cw), 1)
    tri = idx.astype(_BF16)
    heads = _ATTN_HEADS
    lanes = heads * _HEAD_DIM
    return pl.pallas_call(
        functools.partial(_attn_kernel, bq=bq, bk=bk, off=off, heads=heads),
        grid=(b, nh // heads, tq // bq),
        in_specs=[
            pl.BlockSpec((None, bq, lanes), lambda i, h, j: (i, j, h)),
            pl.BlockSpec((None, tk, lanes), lambda i, h, j: (i, 0, h)),
            pl.BlockSpec((None, tk, lanes), lambda i, h, j: (i, 0, h)),
            pl.BlockSpec((cw, cw), lambda i, h, j: (0, 0)),
        ],
        out_specs=pl.BlockSpec((None, bq, lanes), lambda i, h, j: (i, j, h)),
        out_shape=jax.ShapeDtypeStruct((b, tq, dsb), _BF16),
        compiler_params=_cparams("arbitrary", "arbitrary", "arbitrary"),
        name="attention",
    )(q, k, v, tri)


def _pool_kernel(p_ref, prev_ref, hist_ref, wp_ref, sp_ref, o_ref, xcat_ref, *, bt, off, gd):
    j = pl.program_id(1)
    xcat_ref[0:_HALO, :] = jnp.where(j == 0, hist_ref[...], prev_ref[...])
    xcat_ref[_HALO:, :] = p_ref[...]
    pos = off + j * bt + lax.broadcasted_iota(jnp.int32, (bt, 1), 0)
    for g, win in enumerate(_POOL_WINDOWS):
        cols = slice(g * gd, (g + 1) * gd)
        tot = xcat_ref[_HALO:_HALO + bt, cols]
        for back in range(1, win):
            tot = tot + xcat_ref[_HALO - back:_HALO - back + bt, cols]
        cnt = jnp.minimum(win, pos + 1).astype(_F32)
        diff = (tot / cnt - p_ref[:, cols]).astype(_BF16)
        y = _dot(diff, wp_ref[g].astype(_BF16))
        o_ref[:, cols] = (y * sp_ref[:, cols]).astype(o_ref.dtype)


def _pool(p, hist, off, w_pool, s_pool, l):
    b, seq, dp = p.shape
    depth, ng, gd, _ = w_pool.shape
    bt = min(_ROW_TILE, seq)
    per = bt // _HALO
    return pl.pallas_call(
        functools.partial(_pool_kernel, bt=bt, off=off, gd=gd),
        grid=(b, seq // bt),
        in_specs=[
            pl.BlockSpec((None, bt, dp), lambda i, j: (i, j, 0)),
            pl.BlockSpec((None, _HALO, dp), lambda i, j: (i, jnp.maximum(j * per - 1, 0), 0)),
            pl.BlockSpec((None, _HALO, dp), lambda i, j: (i, 0, 0)),
            pl.BlockSpec((None, ng, gd, gd), lambda i, j: (l, 0, 0, 0)),
            pl.BlockSpec((None, 1, dp), lambda i, j: (l, 0, 0)),
        ],
        out_specs=pl.BlockSpec((None, bt, dp), lambda i, j: (i, j, 0)),
        out_shape=jax.ShapeDtypeStruct((b, seq, dp), _BF16),
        scratch_shapes=[pltpu.VMEM((bt + _HALO, dp), _F32)],
        compiler_params=_cparams("arbitrary", "arbitrary"),
        name="pool",
    )(p, p, hist, w_pool, s_pool.reshape(depth, 1, dp))


def _spatial_kernel(u_ref, vn_ref, wsp_ref, bsp_ref, o_ref, *, bt, lc, gd):
    lower = lax.broadcasted_iota(jnp.int32, (lc, lc), 0) >= lax.broadcasted_iota(jnp.int32, (lc, lc), 1)
    for g in range(_GMLP_GROUPS):
        cols = slice(g * gd, (g + 1) * gd)
        wm = jnp.where(lower, wsp_ref[g, :lc, :lc], 0.0).astype(_BF16)
        bias = bsp_ref[:lc, g:g + 1]
        for c in range(bt // lc):
            rows = slice(c * lc, (c + 1) * lc)
            sv = _dot(wm, vn_ref[rows, cols]) + bias
            o_ref[rows, cols] = (u_ref[rows, cols].astype(_F32) * sv).astype(o_ref.dtype)


def _spatial(u, vn, w_sp, b_sp_t, l):
    b, seq, dg = u.shape
    lc = _GMLP_CHUNK if seq % _GMLP_CHUNK == 0 else seq
    bt = min(_ROW_TILE, seq)
    gd = dg // _GMLP_GROUPS
    return pl.pallas_call(
        functools.partial(_spatial_kernel, bt=bt, lc=lc, gd=gd),
        grid=(b, seq // bt),
        in_specs=[
            pl.BlockSpec((None, bt, dg), lambda i, j: (i, j, 0)),
            pl.BlockSpec((None, bt, dg), lambda i, j: (i, j, 0)),
            pl.BlockSpec((None, _GMLP_GROUPS, _GMLP_CHUNK, _GMLP_CHUNK), lambda i, j: (l, 0, 0, 0)),
            pl.BlockSpec((None, _GMLP_CHUNK, _GMLP_GROUPS), lambda i, j: (l, 0, 0)),
        ],
        out_specs=pl.BlockSpec((None, bt, dg), lambda i, j: (i, j, 0)),
        out_shape=jax.ShapeDtypeStruct((b, seq, dg), _BF16),
        compiler_params=_cparams("arbitrary", "arbitrary"),
        name="spatial",
    )(u, vn, w_sp, b_sp_t)


def _trunk_layer(x, mod, k_past, v_past, hist, off, l, prm, want_gv):
    b, seq, d = x.shape
    m = b * seq
    dsb = d // 2
    dp = d // 4
    dg = d // 4
    sh1, sc1, gt1, sh2, sc2, gt2 = [mod[:, i * d:(i + 1) * d] for i in range(_N_MOD)]

    def sds(n, dt):
        return jax.ShapeDtypeStruct((m, n), dt)

    def no_extra(bm, bn):
        return []

    def head_gain(bm, bn):
        return [pl.BlockSpec((None, 1, _HEAD_DIM), lambda j, i: (l, 0, 0))]

    def two(bm, bn):
        return [_tile_spec(bm, bn), _tile_spec(bm, bn)]

    h = _norm_mod(x, prm["g_n1"], l, sc1, sh1).reshape(m, d)
    w_in = prm["w_in"]
    depth = w_in.shape[0]
    q = _mm_call(_qk_kernel, h, w_in, l, 0, dsb, [prm["g_q"].reshape(depth, 1, _HEAD_DIM)], head_gain,
                 sds(dsb, _BF16), _tile_spec, "proj_q")
    k32, k16 = _mm_call(_qk_kernel, h, w_in, l, dsb, dsb, [prm["g_k"].reshape(depth, 1, _HEAD_DIM)], head_gain,
                        [sds(dsb, _F32), sds(dsb, _BF16)], two, "proj_k")
    v32, v16 = _mm_call(_copy_kernel, h, w_in, l, 2 * dsb, dsb, [], no_extra,
                        [sds(dsb, _F32), sds(dsb, _BF16)], two, "proj_v")
    p = _mm_call(_copy_kernel, h, w_in, l, 3 * dsb, dp, [], no_extra, sds(dp, _F32), _tile_spec, "proj_p")
    u = _mm_call(_gelu_kernel, h, w_in, l, 3 * dsb + dp, dg, [], no_extra, sds(dg, _BF16), _tile_spec, "proj_u")

    def gv_gain(bm, bn):
        return [pl.BlockSpec((None, 1, dg), lambda j, i: (l, 0, 0))]

    gv_in = [prm["g_v"].reshape(depth, 1, dg)]
    if want_gv:
        gv32, vn = _mm_call(_gelu_norm_kernel, h, w_in, l, 3 * dsb + dp + dg, dg, gv_in, gv_gain,
                            [sds(dg, _F32), sds(dg, _BF16)], two, "proj_gv")
    else:
        gv32 = None
        vn = _mm_call(_gelu_norm_kernel, h, w_in, l, 3 * dsb + dp + dg, dg, gv_in, gv_gain,
                      sds(dg, _BF16), _tile_spec, "proj_gv")

    def gate_bias(bm, bn):
        return [pl.BlockSpec((None, 1, bn), lambda j, i: (l, 0, j))]

    gates = _mm_call(_gates_kernel, h, prm["w_gate"], l, 0, _N_BRANCH * d,
                     [prm["b_gate"].reshape(depth, 1, _N_BRANCH * d)], gate_bias,
                     sds(_N_BRANCH * d, _BF16), _tile_spec, "gates")

    q3 = q.reshape(b, seq, dsb)
    k3 = k16.reshape(b, seq, dsb)
    v3 = v16.reshape(b, seq, dsb)
    if k_past is not None:
        k3 = jnp.concatenate([k_past.reshape(b, off, dsb).astype(_BF16), k3], axis=1)
        v3 = jnp.concatenate([v_past.reshape(b, off, dsb).astype(_BF16), v3], axis=1)
    a = _attention(q3, k3, v3, off).reshape(m, dsb)

    p3 = p.reshape(b, seq, dp)
    pb = _pool(p3, hist, off, prm["w_pool"], prm["s_pool"], l).reshape(m, dp)
    cg = _spatial(u.reshape(b, seq, dg), vn.reshape(b, seq, dg), prm["w_sp"], prm["b_sp_t"], l).reshape(m, dg)

    merged = _merge(a, pb, cg, prm["w_br_a"], prm["w_br_b"], prm["w_br_c"], gates, l)

    def resid(gt):
        def specs(bm, bn):
            return [_tile_spec(bm, bn), pl.BlockSpec((b, bn), lambda j, i: (0, j))]
        return specs

    x1 = _mm_call(functools.partial(_out_kernel, seq=seq), merged, prm["w_out"], l, 0, d,
                  [x.reshape(m, d), gt1], resid(gt1), sds(d, _F32), _tile_spec, "out_proj")
    h2 = _norm_mod(x1.reshape(b, seq, d), prm["g_n2"], l, sc2, sh2).reshape(m, d)
    f = _mm_call(_relu2_kernel, h2, prm["w_ff1"], l, 0, prm["w_ff1"].shape[2], [], no_extra,
                 sds(prm["w_ff1"].shape[2], _BF16), _tile_spec, "ff1")
    x2 = _ff2(f, prm["w_ff2"], l, x1, gt2, seq).reshape(b, seq, d)

    nh = dsb // _HEAD_DIM
    new_k = k32.reshape(b, seq, nh, _HEAD_DIM)
    new_v = v32.reshape(b, seq, nh, _HEAD_DIM)
    new_pool = p3[:, seq - _POOL_HIST:, :]
    new_gv = gv32.reshape(b, seq, dg) if want_gv else None
    return x2, new_k, new_v, new_pool, new_gv


def kernel(x_prompt, x_sample, c_prompt, c_sample, cache_k, cache_v, state_pool, w_ada, b_ada, g_n1, w_in, g_q, g_k, w_pool, s_pool, w_sp, b_sp, g_v, w_br_a, w_br_b, w_br_c, w_gate, b_gate, w_out, g_n2, w_ff1, w_ff2):
    depth = w_in.shape[0]
    nb_p = x_prompt.shape[0]
    nb_s = x_sample.shape[0]
    past = cache_k.shape[2]
    dp = state_pool.shape[-1]
    assert nb_p + nb_s <= _C_ROWS

    prm = {
        "g_n1": g_n1, "g_q": g_q, "g_k": g_k, "g_v": g_v, "g_n2": g_n2,
        "w_pool": w_pool, "s_pool": s_pool, "w_sp": w_sp, "b_sp_t": jnp.swapaxes(b_sp, 1, 2),
        "b_gate": b_gate,
        "w_in": w_in.astype(_BF16), "w_gate": w_gate.astype(_BF16),
        "w_br_a": w_br_a.astype(_BF16), "w_br_b": w_br_b.astype(_BF16), "w_br_c": w_br_c.astype(_BF16),
        "w_out": w_out.astype(_BF16), "w_ff1": w_ff1.astype(_BF16), "w_ff2": w_ff2.astype(_BF16),
    }

    c_all = jnp.concatenate(
        [c_prompt, c_sample, jnp.zeros((_C_ROWS - nb_p - nb_s, c_prompt.shape[1]), _F32)], axis=0)
    mod = _ada(c_all, w_ada, b_ada)

    hist_p = jnp.zeros((nb_p, _HALO, dp), _F32)
    hist_s = jnp.pad(state_pool, ((0, 0), (0, 0), (_HALO - _POOL_HIST, 0), (0, 0)))

    yp, ys = x_prompt, x_sample
    outs = [[] for _ in range(7)]
    for l in range(depth):
        yp, kp, vp, pp, _ = _trunk_layer(yp, mod[l, :nb_p], None, None, hist_p, 0, l, prm, False)
        ys, ks, vs, ps, gs = _trunk_layer(ys, mod[l, nb_p:nb_p + nb_s], cache_k[l], cache_v[l], hist_s[l],
                                          past, l, prm, True)
        for lst, val in zip(outs, (kp, vp, pp, ks, vs, ps, gs)):
            lst.append(val)
    return (yp, ys) + tuple(jnp.stack(o) for o in outs)
```

```python
import functools

import jax
import jax.numpy as jnp
from jax import lax
from jax.experimental import pallas as pl
from jax.experimental.pallas import tpu as pltpu

_EPS = 1e-6
_HEAD_DIM = 128
_POOL_WINDOWS = (2, 4, 8, 16)
_POOL_HIST = max(_POOL_WINDOWS) - 1
_HALO = 16
_GMLP_GROUPS = 4
_GMLP_CHUNK = 128
_N_MOD = 6
_N_BRANCH = 3
_C_ROWS = 16

_VMEM_LIMIT_BYTES = 56 * 1024 * 1024
_BLOCK_M = 1024
_BLOCK_N = 1024
_BLOCK_K = 2048
_ATTN_BLOCK = 256
_ATTN_CHUNK = 256
_UNDERFLOW_LOG2 = 151.0
_ATTN_HEADS = 8
_LOG2E = 1.4426950408889634
_ROW_TILE = 512

_F32 = jnp.float32
_BF16 = jnp.bfloat16


def _cparams(*sem):
    return pltpu.CompilerParams(dimension_semantics=sem, vmem_limit_bytes=_VMEM_LIMIT_BYTES)


def _dot(a, b):
    return jnp.dot(a, b, preferred_element_type=_F32)


def _sigmoid(x):
    return 1.0 / (1.0 + jnp.exp(-x))


def _ada_kernel(c_ref, w_ref, b_ref, o_ref):
    c = c_ref[...]
    s = (c * _sigmoid(c)).astype(_BF16)
    o_ref[...] = _dot(s, w_ref[...].astype(_BF16)) + b_ref[...]


def _ada(c_all, w_ada, b_ada):
    depth, d, n = w_ada.shape
    bn = 512
    return pl.pallas_call(
        _ada_kernel,
        grid=(depth, n // bn),
        in_specs=[
            pl.BlockSpec((_C_ROWS, d), lambda l, j: (0, 0)),
            pl.BlockSpec((None, d, bn), lambda l, j: (l, 0, j)),
            pl.BlockSpec((None, 1, bn), lambda l, j: (l, 0, j)),
        ],
        out_specs=pl.BlockSpec((None, _C_ROWS, bn), lambda l, j: (l, 0, j)),
        out_shape=jax.ShapeDtypeStruct((depth, _C_ROWS, n), _F32),
        compiler_params=_cparams("arbitrary", "arbitrary"),
        name="ada",
    )(c_all, w_ada, b_ada.reshape(depth, 1, n))


def _norm_mod_kernel(x_ref, g_ref, sc_ref, sh_ref, o_ref):
    x = x_ref[...]
    ms = jnp.mean(x * x, axis=-1, keepdims=True)
    y = x * lax.rsqrt(ms + _EPS) * g_ref[...]
    o_ref[...] = (y * (1.0 + sc_ref[...]) + sh_ref[...]).astype(o_ref.dtype)


def _norm_mod(x, g, l, scale, shift):
    b, seq, d = x.shape
    bt = min(_ROW_TILE, seq)
    depth = g.shape[0]
    return pl.pallas_call(
        _norm_mod_kernel,
        grid=(b, seq // bt),
        in_specs=[
            pl.BlockSpec((None, bt, d), lambda i, j: (i, j, 0)),
            pl.BlockSpec((None, 1, d), lambda i, j: (l, 0, 0)),
            pl.BlockSpec((None, 1, d), lambda i, j: (i, 0, 0)),
            pl.BlockSpec((None, 1, d), lambda i, j: (i, 0, 0)),
        ],
        out_specs=pl.BlockSpec((None, bt, d), lambda i, j: (i, j, 0)),
        out_shape=jax.ShapeDtypeStruct((b, seq, d), _BF16),
        compiler_params=_cparams("arbitrary", "arbitrary"),
        name="norm_mod",
    )(x, g.reshape(depth, 1, d), scale.reshape(b, 1, d), shift.reshape(b, 1, d))


def _mm_call(kern, x, w, l, col0, ncols, extra_in, extra_specs, out_shape, out_specs, name):
    m, k = x.shape
    bm = min(_BLOCK_M, m)
    bn = min(_BLOCK_N, ncols)
    cb = col0 // bn
    in_specs = [
        pl.BlockSpec((bm, k), lambda j, i: (i, 0)),
        pl.BlockSpec((None, k, bn), lambda j, i: (l, 0, j + cb)),
    ] + list(extra_specs(bm, bn))
    return pl.pallas_call(
        kern,
        grid=(ncols // bn, m // bm),
        in_specs=in_specs,
        out_specs=out_specs(bm, bn),
        out_shape=out_shape,
        compiler_params=_cparams("arbitrary", "arbitrary"),
        name=name,
    )(x, w, *extra_in)


def _tile_spec(bm, bn):
    return pl.BlockSpec((bm, bn), lambda j, i: (i, j))


def _head_norm_store(acc, g, out_refs):
    for h in range(acc.shape[1] // _HEAD_DIM):
        cols = slice(h * _HEAD_DIM, (h + 1) * _HEAD_DIM)
        a = acc[:, cols]
        ms = jnp.mean(a * a, axis=-1, keepdims=True)
        y = a * lax.rsqrt(ms + _EPS) * g
        for o_ref in out_refs:
            o_ref[:, cols] = y.astype(o_ref.dtype)


def _qk_kernel(x_ref, w_ref, g_ref, *out_refs):
    _head_norm_store(_dot(x_ref[...], w_ref[...]), g_ref[...], out_refs)


def _copy_kernel(x_ref, w_ref, *out_refs):
    acc = _dot(x_ref[...], w_ref[...])
    for o_ref in out_refs:
        o_ref[...] = acc.astype(o_ref.dtype)


def _gelu_kernel(x_ref, w_ref, o_ref):
    o_ref[...] = jax.nn.gelu(_dot(x_ref[...], w_ref[...])).astype(o_ref.dtype)


def _gelu_norm_kernel(x_ref, w_ref, g_ref, *out_refs):
    v = jax.nn.gelu(_dot(x_ref[...], w_ref[...]))
    ms = jnp.mean(v * v, axis=-1, keepdims=True)
    y = v * lax.rsqrt(ms + _EPS) * g_ref[...]
    for o_ref in out_refs:
        o_ref[...] = y.astype(o_ref.dtype)


def _gates_kernel(x_ref, w_ref, b_ref, o_ref):
    o_ref[...] = _sigmoid(_dot(x_ref[...], w_ref[...]) + b_ref[...]).astype(o_ref.dtype)


def _relu2_kernel(x_ref, w_ref, o_ref):
    r = jnp.maximum(_dot(x_ref[...], w_ref[...]), 0.0)
    o_ref[...] = (r * r).astype(o_ref.dtype)


def _gated_residual(x_ref, gt_ref, acc, o_ref, i, seq):
    bm = acc.shape[0]
    if seq >= bm:
        b = (i * bm) // seq
        o_ref[...] = x_ref[...] + gt_ref[pl.ds(b, 1), :] * acc
    else:
        per = bm // seq
        for s in range(per):
            rows = slice(s * seq, (s + 1) * seq)
            o_ref[rows, :] = x_ref[rows, :] + gt_ref[pl.ds(i * per + s, 1), :] * acc[rows, :]


def _out_kernel(m_ref, w_ref, x_ref, gt_ref, o_ref, *, seq):
    _gated_residual(x_ref, gt_ref, _dot(m_ref[...], w_ref[...]), o_ref, pl.program_id(1), seq)


def _merge_kernel(a_ref, pb_ref, cg_ref, wa_ref, wb_ref, wc_ref, ga_ref, gb_ref, gc_ref, o_ref):
    ya = _dot(a_ref[...], wa_ref[...])
    yb = _dot(pb_ref[...], wb_ref[...])
    yc = _dot(cg_ref[...], wc_ref[...])
    o = ga_ref[...].astype(_F32) * ya + gb_ref[...].astype(_F32) * yb + gc_ref[...].astype(_F32) * yc
    o_ref[...] = o.astype(o_ref.dtype)


def _merge(a, pb, cg, wa, wb, wc, gates, l):
    m = a.shape[0]
    d = wa.shape[2]
    bm = min(_BLOCK_M // 2, m)
    bn = min(_BLOCK_N, d)
    nb = d // bn

    def act(x):
        return pl.BlockSpec((bm, x.shape[1]), lambda j, i: (i, 0))

    def wt(w):
        return pl.BlockSpec((None, w.shape[1], bn), lambda j, i: (l, 0, j))

    def gate(branch):
        return pl.BlockSpec((bm, bn), lambda j, i: (i, j + branch * nb))

    return pl.pallas_call(
        _merge_kernel,
        grid=(nb, m // bm),
        in_specs=[act(a), act(pb), act(cg), wt(wa), wt(wb), wt(wc), gate(0), gate(1), gate(2)],
        out_specs=_tile_spec(bm, bn),
        out_shape=jax.ShapeDtypeStruct((m, d), _BF16),
        compiler_params=_cparams("arbitrary", "arbitrary"),
        name="merge",
    )(a, pb, cg, wa, wb, wc, gates, gates, gates)


def _ff2_kernel(f_ref, w_ref, x_ref, gt_ref, o_ref, *, seq, nk):
    k = pl.program_id(2)

    @pl.when(k == 0)
    def _():
        o_ref[...] = jnp.zeros_like(o_ref)

    o_ref[...] += _dot(f_ref[...], w_ref[...])

    @pl.when(k == nk - 1)
    def _():
        _gated_residual(x_ref, gt_ref, o_ref[...], o_ref, pl.program_id(1), seq)


def _ff2(f, w, l, x, gt, seq):
    m, kdim = f.shape
    d = w.shape[2]
    bm = min(_BLOCK_M, m)
    bn = min(_BLOCK_N, d)
    bk = min(_BLOCK_K, kdim)
    nk = kdim // bk
    nbatch = gt.shape[0]
    return pl.pallas_call(
        functools.partial(_ff2_kernel, seq=seq, nk=nk),
        grid=(d // bn, m // bm, nk),
        in_specs=[
            pl.BlockSpec((bm, bk), lambda j, i, k: (i, k)),
            pl.BlockSpec((None, bk, bn), lambda j, i, k: (l, k, j)),
            pl.BlockSpec((bm, bn), lambda j, i, k: (i, j)),
            pl.BlockSpec((nbatch, bn), lambda j, i, k: (0, j)),
        ],
        out_specs=pl.BlockSpec((bm, bn), lambda j, i, k: (i, j)),
        out_shape=jax.ShapeDtypeStruct((m, d), _F32),
        compiler_params=_cparams("arbitrary", "arbitrary", "arbitrary"),
        name="ff2",
    )(f, w, x, gt)


def _attn_blocks(q_ref, k_ref, v_ref, tri, cols, rows, runs, mask):
    scores = [lax.dot_general(q_ref[:, c], k_ref[rows, c], (((1,), (1,)), ((), ())),
                              preferred_element_type=_F32) for c in cols]
    logits, sps, sums = [], [], []
    for s in scores:
        s = s * (_HEAD_DIM ** -0.5 * _LOG2E)
        neg_abs = pltpu.bitcast(pltpu.bitcast(s, jnp.uint32) | jnp.uint32(0x80000000), _F32)
        sp = jnp.maximum(s, 0.0) + jnp.log(1.0 + jnp.exp2(neg_abs)) * _LOG2E
        logits.append(s - sp)
        if mask is not None:
            sp = jnp.where(mask, sp, 0.0)
        sps.append(sp)
    n = sps[0].shape[1]
    cw = tri.shape[0]
    chunks = [slice(c * cw, (c + 1) * cw) for c in range(n // cw)]
    sufs, offsets = [], []
    for h, sp in enumerate(sps):
        sufs.append([_dot(sp[:, c].astype(_BF16), tri) for c in chunks])
        tail = None
        offs = [None] * len(chunks)
        for i in reversed(range(len(chunks))):
            if runs is None:
                offs[i] = tail
            else:
                offs[i] = runs[h] if tail is None else runs[h] + tail
            rs = jnp.sum(sp[:, chunks[i]], axis=-1, keepdims=True)
            tail = rs if tail is None else tail + rs
        offsets.append(offs)
        sums.append(tail)
    ws = []
    for s, suf, offs in zip(logits, sufs, offsets):
        parts = []
        for c, sf, of in zip(chunks, suf, offs):
            a = s[:, c] - sf
            parts.append(jnp.exp2(a if of is None else a - of))
        w = parts[0] if len(parts) == 1 else jnp.concatenate(parts, axis=1)
        if mask is not None:
            w = jnp.where(mask, w, 0.0)
        ws.append(w.astype(_BF16))
    pvs = [_dot(w, v_ref[rows, c]) for w, c in zip(ws, cols)]
    return pvs, sums


def _attn_kernel(q_ref, k_ref, v_ref, tri_ref, o_ref, *, bq, bk, off, heads):
    row0 = off + pl.program_id(2) * bq
    start = pl.multiple_of(row0, bq)
    mask = lax.broadcasted_iota(jnp.int32, (bq, bq), 1) < lax.broadcasted_iota(jnp.int32, (bq, bq), 0)
    cols = [slice(h * _HEAD_DIM, (h + 1) * _HEAD_DIM) for h in range(heads)]

    def settled(runs):
        low = runs[0]
        for r in runs[1:]:
            low = jnp.minimum(low, r)
        return jnp.min(low) >= _UNDERFLOW_LOG2

    dw = min(bq, tri_ref.shape[0])
    accs, runs = _attn_blocks(q_ref, k_ref, v_ref, tri_ref[:dw, :dw], cols, pl.ds(start, bq), None, mask)
    nblk = row0 // bk

    def cond(carry):
        step, done, _, _ = carry
        return jnp.logical_and(step < nblk, jnp.logical_not(done))

    def body(carry):
        step, _, accs, runs = carry
        kstart = pl.multiple_of(row0 - (step + 1) * bk, bk)
        pvs, sums = _attn_blocks(q_ref, k_ref, v_ref, tri_ref[...], cols, pl.ds(kstart, bk), runs, None)
        runs = tuple(r + t for r, t in zip(runs, sums))
        return step + 1, settled(runs), tuple(a + p for a, p in zip(accs, pvs)), runs

    _, _, accs, _ = lax.while_loop(cond, body, (0, settled(runs), tuple(accs), tuple(runs)))
    for c, acc in zip(cols, accs):
        o_ref[:, c] = acc.astype(o_ref.dtype)


def _attention(q, k, v, off):
    b, tq, dsb = q.shape
    tk = k.shape[1]
    nh = dsb // _HEAD_DIM
    bk = _ATTN_BLOCK
    bq = min(_ATTN_BLOCK, tq)
    cw = min(_ATTN_CHUNK, bk)
    assert off % bk == 0 and (bq == bk or tq == bq) and bk % cw == 0 and (bq % cw == 0 or bq < cw)
    idx = lax.broadcasted_iota(jnp.int32, (cw, cw), 0) > lax.broadcasted_iota(jnp.int32, (cw, cw), 1)
    tri = idx.astype(_BF16)
    heads = _ATTN_HEADS
    lanes = heads * _HEAD_DIM
    return pl.pallas_call(
        functools.partial(_attn_kernel, bq=bq, bk=bk, off=off, heads=heads),
        grid=(b, nh // heads, tq // bq),
        in_specs=[
            pl.BlockSpec((None, bq, lanes), lambda i, h, j: (i, j, h)),
            pl.BlockSpec((None, tk, lanes), lambda i, h, j: (i, 0, h)),
            pl.BlockSpec((None, tk, lanes), lambda i, h, j: (i, 0, h)),
            pl.BlockSpec((cw, cw), lambda i, h, j: (0, 0)),
        ],
        out_specs=pl.BlockSpec((None, bq, lanes), lambda i, h, j: (i, j, h)),
        out_shape=jax.ShapeDtypeStruct((b, tq, dsb), _BF16),
        compiler_params=_cparams("arbitrary", "arbitrary", "arbitrary"),
        name="attention",
    )(q, k, v, tri)


def _pool_kernel(p_ref, prev_ref, hist_ref, wp_ref, sp_ref, o_ref, xcat_ref, *, bt, off, gd):
    j = pl.program_id(1)
    xcat_ref[0:_HALO, :] = jnp.where(j == 0, hist_ref[...], prev_ref[...])
    xcat_ref[_HALO:, :] = p_ref[...]
    pos = off + j * bt + lax.broadcasted_iota(jnp.int32, (bt, 1), 0)
    for g, win in enumerate(_POOL_WINDOWS):
        cols = slice(g * gd, (g + 1) * gd)
        tot = xcat_ref[_HALO:_HALO + bt, cols]
        for back in range(1, win):
            tot = tot + xcat_ref[_HALO - back:_HALO - back + bt, cols]
        cnt = jnp.minimum(win, pos + 1).astype(_F32)
        diff = (tot / cnt - p_ref[:, cols]).astype(_BF16)
        y = _dot(diff, wp_ref[g].astype(_BF16))
        o_ref[:, cols] = (y * sp_ref[:, cols]).astype(o_ref.dtype)


def _pool(p, hist, off, w_pool, s_pool, l):
    b, seq, dp = p.shape
    depth, ng, gd, _ = w_pool.shape
    bt = min(_ROW_TILE, seq)
    per = bt // _HALO
    return pl.pallas_call(
        functools.partial(_pool_kernel, bt=bt, off=off, gd=gd),
        grid=(b, seq // bt),
        in_specs=[
            pl.BlockSpec((None, bt, dp), lambda i, j: (i, j, 0)),
            pl.BlockSpec((None, _HALO, dp), lambda i, j: (i, jnp.maximum(j * per - 1, 0), 0)),
            pl.BlockSpec((None, _HALO, dp), lambda i, j: (i, 0, 0)),
            pl.BlockSpec((None, ng, gd, gd), lambda i, j: (l, 0, 0, 0)),
            pl.BlockSpec((None, 1, dp), lambda i, j: (l, 0, 0)),
        ],
        out_specs=pl.BlockSpec((None, bt, dp), lambda i, j: (i, j, 0)),
        out_shape=jax.ShapeDtypeStruct((b, seq, dp), _BF16),
        scratch_shapes=[pltpu.VMEM((bt + _HALO, dp), _F32)],
        compiler_params=_cparams("arbitrary", "arbitrary"),
        name="pool",
    )(p, p, hist, w_pool, s_pool.reshape(depth, 1, dp))


def _spatial_kernel(u_ref, vn_ref, wsp_ref, bsp_ref, o_ref, *, bt, lc, gd):
    lower = lax.broadcasted_iota(jnp.int32, (lc, lc), 0) >= lax.broadcasted_iota(jnp.int32, (lc, lc), 1)
    for g in range(_GMLP_GROUPS):
        cols = slice(g * gd, (g + 1) * gd)
        wm = jnp.where(lower, wsp_ref[g, :lc, :lc], 0.0).astype(_BF16)
        bias = bsp_ref[:lc, g:g + 1]
        for c in range(bt // lc):
            rows = slice(c * lc, (c + 1) * lc)
            sv = _dot(wm, vn_ref[rows, cols]) + bias
            o_ref[rows, cols] = (u_ref[rows, cols].astype(_F32) * sv).astype(o_ref.dtype)


def _spatial(u, vn, w_sp, b_sp_t, l):
    b, seq, dg = u.shape
    lc = _GMLP_CHUNK if seq % _GMLP_CHUNK == 0 else seq
    bt = min(_ROW_TILE, seq)
    gd = dg // _GMLP_GROUPS
    return pl.pallas_call(
        functools.partial(_spatial_kernel, bt=bt, lc=lc, gd=gd),
        grid=(b, seq // bt),
        in_specs=[
            pl.BlockSpec((None, bt, dg), lambda i, j: (i, j, 0)),
            pl.BlockSpec((None, bt, dg), lambda i, j: (i, j, 0)),
            pl.BlockSpec((None, _GMLP_GROUPS, _GMLP_CHUNK, _GMLP_CHUNK), lambda i, j: (l, 0, 0, 0)),
            pl.BlockSpec((None, _GMLP_CHUNK, _GMLP_GROUPS), lambda i, j: (l, 0, 0)),
        ],
        out_specs=pl.BlockSpec((None, bt, dg), lambda i, j: (i, j, 0)),
        out_shape=jax.ShapeDtypeStruct((b, seq, dg), _BF16),
        compiler_params=_cparams("arbitrary", "arbitrary"),
        name="spatial",
    )(u, vn, w_sp, b_sp_t)


def _trunk_layer(x, mod, k_past, v_past, hist, off, l, prm, want_gv):
    b, seq, d = x.shape
    m = b * seq
    dsb = d // 2
    dp = d // 4
    dg = d // 4
    sh1, sc1, gt1, sh2, sc2, gt2 = [mod[:, i * d:(i + 1) * d] for i in range(_N_MOD)]

    def sds(n, dt):
        return jax.ShapeDtypeStruct((m, n), dt)

    def no_extra(bm, bn):
        return []

    def head_gain(bm, bn):
        return [pl.BlockSpec((None, 1, _HEAD_DIM), lambda j, i: (l, 0, 0))]

    def two(bm, bn):
        return [_tile_spec(bm, bn), _tile_spec(bm, bn)]

    h = _norm_mod(x, prm["g_n1"], l, sc1, sh1).reshape(m, d)
    w_in = prm["w_in"]
    depth = w_in.shape[0]
    q = _mm_call(_qk_kernel, h, w_in, l, 0, dsb, [prm["g_q"].reshape(depth, 1, _HEAD_DIM)], head_gain,
                 sds(dsb, _BF16), _tile_spec, "proj_q")
    k32, k16 = _mm_call(_qk_kernel, h, w_in, l, dsb, dsb, [prm["g_k"].reshape(depth, 1, _HEAD_DIM)], head_gain,
                        [sds(dsb, _F32), sds(dsb, _BF16)], two, "proj_k")
    v32, v16 = _mm_call(_copy_kernel, h, w_in, l, 2 * dsb, dsb, [], no_extra,
                        [sds(dsb, _F32), sds(dsb, _BF16)], two, "proj_v")
    p = _mm_call(_copy_kernel, h, w_in, l, 3 * dsb, dp, [], no_extra, sds(dp, _F32), _tile_spec, "proj_p")
    u = _mm_call(_gelu_kernel, h, w_in, l, 3 * dsb + dp, dg, [], no_extra, sds(dg, _BF16), _tile_spec, "proj_u")

    def gv_gain(bm, bn):
        return [pl.BlockSpec((None, 1, dg), lambda j, i: (l, 0, 0))]

    gv_in = [prm["g_v"].reshape(depth, 1, dg)]
    if want_gv:
        gv32, vn = _mm_call(_gelu_norm_kernel, h, w_in, l, 3 * dsb + dp + dg, dg, gv_in, gv_gain,
                            [sds(dg, _F32), sds(dg, _BF16)], two, "proj_gv")
    else:
        gv32 = None
        vn = _mm_call(_gelu_norm_kernel, h, w_in, l, 3 * dsb + dp + dg, dg, gv_in, gv_gain,
                      sds(dg, _BF16), _tile_spec, "proj_gv")

    def gate_bias(bm, bn):
        return [pl.BlockSpec((None, 1, bn), lambda j, i: (l, 0, j))]

    gates = _mm_call(_gates_kernel, h, prm["w_gate"], l, 0, _N_BRANCH * d,
                     [prm["b_gate"].reshape(depth, 1, _N_BRANCH * d)], gate_bias,
                     sds(_N_BRANCH * d, _BF16), _tile_spec, "gates")

    q3 = q.reshape(b, seq, dsb)
    k3 = k16.reshape(b, seq, dsb)
    v3 = v16.reshape(b, seq, dsb)
    if k_past is not None:
        k3 = jnp.concatenate([k_past.reshape(b, off, dsb).astype(_BF16), k3], axis=1)
        v3 = jnp.concatenate([v_past.reshape(b, off, dsb).astype(_BF16), v3], axis=1)
    a = _attention(q3, k3, v3, off).reshape(m, dsb)

    p3 = p.reshape(b, seq, dp)
    pb = _pool(p3, hist, off, prm["w_pool"], prm["s_pool"], l).reshape(m, dp)
    cg = _spatial(u.reshape(b, seq, dg), vn.reshape(b, seq, dg), prm["w_sp"], prm["b_sp_t"], l).reshape(m, dg)

    merged = _merge(a, pb, cg, prm["w_br_a"], prm["w_br_b"], prm["w_br_c"], gates, l)

    def resid(gt):
        def specs(bm, bn):
            return [_tile_spec(bm, bn), pl.BlockSpec((b, bn), lambda j, i: (0, j))]
        return specs

    x1 = _mm_call(functools.partial(_out_kernel, seq=seq), merged, prm["w_out"], l, 0, d,
                  [x.reshape(m, d), gt1], resid(gt1), sds(d, _F32), _tile_spec, "out_proj")
    h2 = _norm_mod(x1.reshape(b, seq, d), prm["g_n2"], l, sc2, sh2).reshape(m, d)
    f = _mm_call(_relu2_kernel, h2, prm["w_ff1"], l, 0, prm["w_ff1"].shape[2], [], no_extra,
                 sds(prm["w_ff1"].shape[2], _BF16), _tile_spec, "ff1")
    x2 = _ff2(f, prm["w_ff2"], l, x1, gt2, seq).reshape(b, seq, d)

    nh = dsb // _HEAD_DIM
    new_k = k32.reshape(b, seq, nh, _HEAD_DIM)
    new_v = v32.reshape(b, seq, nh, _HEAD_DIM)
    new_pool = p3[:, seq - _POOL_HIST:, :]
    new_gv = gv32.reshape(b, seq, dg) if want_gv else None
    return x2, new_k, new_v, new_pool, new_gv


def kernel(x_prompt, x_sample, c_prompt, c_sample, cache_k, cache_v, state_pool, w_ada, b_ada, g_n1, w_in, g_q, g_k, w_pool, s_pool, w_sp, b_sp, g_v, w_br_a, w_br_b, w_br_c, w_gate, b_gate, w_out, g_n2, w_ff1, w_ff2):
    depth = w_in.shape[0]
    nb_p = x_prompt.shape[0]
    nb_s = x_sample.shape[0]
    past = cache_k.shape[2]
    dp = state_pool.shape[-1]
    assert nb_p + nb_s <= _C_ROWS

    prm = {
        "g_n1": g_n1, "g_q": g_q, "g_k": g_k, "g_v": g_v, "g_n2": g_n2,
        "w_pool": w_pool, "s_pool": s_pool, "w_sp": w_sp, "b_sp_t": jnp.swapaxes(b_sp, 1, 2),
        "b_gate": b_gate,
        "w_in": w_in.astype(_BF16), "w_gate": w_gate.astype(_BF16),
        "w_br_a": w_br_a.astype(_BF16), "w_br_b": w_br_b.astype(_BF16), "w_br_c": w_br_c.astype(_BF16),
        "w_out": w_out.astype(_BF16), "w_ff1": w_ff1.astype(_BF16), "w_ff2": w_ff2.astype(_BF16),
    }

    c_all = jnp.concatenate(
        [c_prompt, c_sample, jnp.zeros((_C_ROWS - nb_p - nb_s, c_prompt.shape[1]), _F32)], axis=0)
    mod = _ada(c_all, w_ada, b_ada)

    hist_p = jnp.zeros((nb_p, _HALO, dp), _F32)
    hist_s = jnp.pad(state_pool, ((0, 0), (0, 0), (_HALO - _POOL_HIST, 0), (0, 0)))

    yp, ys = x_prompt, x_sample
    outs = [[] for _ in range(7)]
    for l in range(depth):
        yp, kp, vp, pp, _ = _trunk_layer(yp, mod[l, :nb_p], None, None, hist_p, 0, l, prm, False)
        ys, ks, vs, ps, gs = _trunk_layer(ys, mod[l, nb_p:nb_p + nb_s], cache_k[l], cache_v[l], hist_s[l],
                                          past, l, prm, True)
        for lst, val in zip(outs, (kp, vp, pp, ks, vs, ps, gs)):
            lst.append(val)
    return (yp, ys) + tuple(jnp.stack(o) for o in outs)
```

```python
import functools

import jax
import jax.numpy as jnp
from jax import lax
from jax.experimental import pallas as pl
from jax.experimental.pallas import tpu as pltpu

_EPS = 1e-6
_HEAD_DIM = 128
_POOL_WINDOWS = (2, 4, 8, 16)
_POOL_HIST = max(_POOL_WINDOWS) - 1
_HALO = 16
_GMLP_GROUPS = 4
_GMLP_CHUNK = 128
_N_MOD = 6
_N_BRANCH = 3
_C_ROWS = 16

_VMEM_LIMIT_BYTES = 56 * 1024 * 1024
_BLOCK_M = 1024
_BLOCK_N = 1024
_BLOCK_K = 2048
_ATTN_BLOCK = 256
_ATTN_CHUNK = 256
_UNDERFLOW_LOG2 = 151.0
_ATTN_HEADS = 8
_LOG2E = 1.4426950408889634
_ROW_TILE = 512

_F32 = jnp.float32
_BF16 = jnp.bfloat16


def _cparams(*sem):
    return pltpu.CompilerParams(dimension_semantics=sem, vmem_limit_bytes=_VMEM_LIMIT_BYTES)


def _dot(a, b):
    return jnp.dot(a, b, preferred_element_type=_F32)


def _sigmoid(x):
    return 0.5 * jnp.tanh(0.5 * x) + 0.5


def _ada_kernel(c_ref, w_ref, b_ref, o_ref):
    c = c_ref[...]
    s = (c * _sigmoid(c)).astype(_BF16)
    o_ref[...] = _dot(s, w_ref[...].astype(_BF16)) + b_ref[...]


def _ada(c_all, w_ada, b_ada):
    depth, d, n = w_ada.shape
    bn = 512
    return pl.pallas_call(
        _ada_kernel,
        grid=(depth, n // bn),
        in_specs=[
            pl.BlockSpec((_C_ROWS, d), lambda l, j: (0, 0)),
            pl.BlockSpec((None, d, bn), lambda l, j: (l, 0, j)),
            pl.BlockSpec((None, 1, bn), lambda l, j: (l, 0, j)),
        ],
        out_specs=pl.BlockSpec((None, _C_ROWS, bn), lambda l, j: (l, 0, j)),
        out_shape=jax.ShapeDtypeStruct((depth, _C_ROWS, n), _F32),
        compiler_params=_cparams("arbitrary", "arbitrary"),
        name="ada",
    )(c_all, w_ada, b_ada.reshape(depth, 1, n))


def _norm_mod_kernel(x_ref, g_ref, sc_ref, sh_ref, o_ref):
    x = x_ref[...]
    ms = jnp.mean(x * x, axis=-1, keepdims=True)
    y = x * lax.rsqrt(ms + _EPS) * g_ref[...]
    o_ref[...] = (y * (1.0 + sc_ref[...]) + sh_ref[...]).astype(o_ref.dtype)


def _norm_mod(x, g, l, scale, shift):
    b, seq, d = x.shape
    bt = min(_ROW_TILE, seq)
    depth = g.shape[0]
    return pl.pallas_call(
        _norm_mod_kernel,
        grid=(b, seq // bt),
        in_specs=[
            pl.BlockSpec((None, bt, d), lambda i, j: (i, j, 0)),
            pl.BlockSpec((None, 1, d), lambda i, j: (l, 0, 0)),
            pl.BlockSpec((None, 1, d), lambda i, j: (i, 0, 0)),
            pl.BlockSpec((None, 1, d), lambda i, j: (i, 0, 0)),
        ],
        out_specs=pl.BlockSpec((None, bt, d), lambda i, j: (i, j, 0)),
        out_shape=jax.ShapeDtypeStruct((b, seq, d), _BF16),
        compiler_params=_cparams("arbitrary", "arbitrary"),
        name="norm_mod",
    )(x, g.reshape(depth, 1, d), scale.reshape(b, 1, d), shift.reshape(b, 1, d))


def _mm_call(kern, x, w, l, col0, ncols, extra_in, extra_specs, out_shape, out_specs, name):
    m, k = x.shape
    bm = min(_BLOCK_M, m)
    bn = min(_BLOCK_N, ncols)
    cb = col0 // bn
    in_specs = [
        pl.BlockSpec((bm, k), lambda j, i: (i, 0)),
        pl.BlockSpec((None, k, bn), lambda j, i: (l, 0, j + cb)),
    ] + list(extra_specs(bm, bn))
    return pl.pallas_call(
        kern,
        grid=(ncols // bn, m // bm),
        in_specs=in_specs,
        out_specs=out_specs(bm, bn),
        out_shape=out_shape,
        compiler_params=_cparams("arbitrary", "arbitrary"),
        name=name,
    )(x, w, *extra_in)


def _tile_spec(bm, bn):
    return pl.BlockSpec((bm, bn), lambda j, i: (i, j))


def _head_norm_store(acc, g, out_refs):
    for h in range(acc.shape[1] // _HEAD_DIM):
        cols = slice(h * _HEAD_DIM, (h + 1) * _HEAD_DIM)
        a = acc[:, cols]
        ms = jnp.mean(a * a, axis=-1, keepdims=True)
        y = a * lax.rsqrt(ms + _EPS) * g
        for o_ref in out_refs:
            o_ref[:, cols] = y.astype(o_ref.dtype)


def _qk_kernel(x_ref, w_ref, g_ref, *out_refs):
    _head_norm_store(_dot(x_ref[...], w_ref[...]), g_ref[...], out_refs)


def _copy_kernel(x_ref, w_ref, *out_refs):
    acc = _dot(x_ref[...], w_ref[...])
    for o_ref in out_refs:
        o_ref[...] = acc.astype(o_ref.dtype)


def _gelu_kernel(x_ref, w_ref, o_ref):
    o_ref[...] = jax.nn.gelu(_dot(x_ref[...], w_ref[...])).astype(o_ref.dtype)


def _gelu_norm_kernel(x_ref, w_ref, g_ref, *out_refs):
    v = jax.nn.gelu(_dot(x_ref[...], w_ref[...]))
    ms = jnp.mean(v * v, axis=-1, keepdims=True)
    y = v * lax.rsqrt(ms + _EPS) * g_ref[...]
    for o_ref in out_refs:
        o_ref[...] = y.astype(o_ref.dtype)


def _gates_kernel(x_ref, w_ref, b_ref, o_ref):
    o_ref[...] = _sigmoid(_dot(x_ref[...], w_ref[...]) + b_ref[...]).astype(o_ref.dtype)


def _relu2_kernel(x_ref, w_ref, o_ref):
    r = jnp.maximum(_dot(x_ref[...], w_ref[...]), 0.0)
    o_ref[...] = (r * r).astype(o_ref.dtype)


def _gated_residual(x_ref, gt_ref, acc, o_ref, i, seq):
    bm = acc.shape[0]
    if seq >= bm:
        b = (i * bm) // seq
        o_ref[...] = x_ref[...] + gt_ref[pl.ds(b, 1), :] * acc
    else:
        per = bm // seq
        for s in range(per):
            rows = slice(s * seq, (s + 1) * seq)
            o_ref[rows, :] = x_ref[rows, :] + gt_ref[pl.ds(i * per + s, 1), :] * acc[rows, :]


def _out_kernel(m_ref, w_ref, x_ref, gt_ref, o_ref, *, seq):
    _gated_residual(x_ref, gt_ref, _dot(m_ref[...], w_ref[...]), o_ref, pl.program_id(1), seq)


def _merge_kernel(a_ref, pb_ref, cg_ref, wa_ref, wb_ref, wc_ref, ga_ref, gb_ref, gc_ref, o_ref):
    ya = _dot(a_ref[...], wa_ref[...])
    yb = _dot(pb_ref[...], wb_ref[...])
    yc = _dot(cg_ref[...], wc_ref[...])
    o = ga_ref[...].astype(_F32) * ya + gb_ref[...].astype(_F32) * yb + gc_ref[...].astype(_F32) * yc
    o_ref[...] = o.astype(o_ref.dtype)


def _merge(a, pb, cg, wa, wb, wc, gates, l):
    m = a.shape[0]
    d = wa.shape[2]
    bm = min(_BLOCK_M // 2, m)
    bn = min(_BLOCK_N, d)
    nb = d // bn

    def act(x):
        return pl.BlockSpec((bm, x.shape[1]), lambda j, i: (i, 0))

    def wt(w):
        return pl.BlockSpec((None, w.shape[1], bn), lambda j, i: (l, 0, j))

    def gate(branch):
        return pl.BlockSpec((bm, bn), lambda j, i: (i, j + branch * nb))

    return pl.pallas_call(
        _merge_kernel,
        grid=(nb, m // bm),
        in_specs=[act(a), act(pb), act(cg), wt(wa), wt(wb), wt(wc), gate(0), gate(1), gate(2)],
        out_specs=_tile_spec(bm, bn),
        out_shape=jax.ShapeDtypeStruct((m, d), _BF16),
        compiler_params=_cparams("arbitrary", "arbitrary"),
        name="merge",
    )(a, pb, cg, wa, wb, wc, gates, gates, gates)


def _ff2_kernel(f_ref, w_ref, x_ref, gt_ref, o_ref, *, seq, nk):
    k = pl.program_id(2)

    @pl.when(k == 0)
    def _():
        o_ref[...] = jnp.zeros_like(o_ref)

    o_ref[...] += _dot(f_ref[...], w_ref[...])

    @pl.when(k == nk - 1)
    def _():
        _gated_residual(x_ref, gt_ref, o_ref[...], o_ref, pl.program_id(1), seq)


def _ff2(f, w, l, x, gt, seq):
    m, kdim = f.shape
    d = w.shape[2]
    bm = min(_BLOCK_M, m)
    bn = min(_BLOCK_N, d)
    bk = min(_BLOCK_K, kdim)
    nk = kdim // bk
    nbatch = gt.shape[0]
    return pl.pallas_call(
        functools.partial(_ff2_kernel, seq=seq, nk=nk),
        grid=(d // bn, m // bm, nk),
        in_specs=[
            pl.BlockSpec((bm, bk), lambda j, i, k: (i, k)),
            pl.BlockSpec((None, bk, bn), lambda j, i, k: (l, k, j)),
            pl.BlockSpec((bm, bn), lambda j, i, k: (i, j)),
            pl.BlockSpec((nbatch, bn), lambda j, i, k: (0, j)),
        ],
        out_specs=pl.BlockSpec((bm, bn), lambda j, i, k: (i, j)),
        out_shape=jax.ShapeDtypeStruct((m, d), _F32),
        compiler_params=_cparams("arbitrary", "arbitrary", "arbitrary"),
        name="ff2",
    )(f, w, x, gt)


def _attn_blocks(q_ref, k_ref, v_ref, tri, cols, rows, runs, mask):
    scores = [lax.dot_general(q_ref[:, c], k_ref[rows, c], (((1,), (1,)), ((), ())),
                              preferred_element_type=_F32) for c in cols]
    logits, sps, sums = [], [], []
    for s in scores:
        s = s * (_HEAD_DIM ** -0.5 * _LOG2E)
        neg_abs = pltpu.bitcast(pltpu.bitcast(s, jnp.uint32) | jnp.uint32(0x80000000), _F32)
        sp = jnp.maximum(s, 0.0) + jnp.log(1.0 + jnp.exp2(neg_abs)) * _LOG2E
        logits.append(s - sp)
        if mask is not None:
            sp = jnp.where(mask, sp, 0.0)
        sps.append(sp)
    n = sps[0].shape[1]
    cw = tri.shape[0]
    chunks = [slice(c * cw, (c + 1) * cw) for c in range(n // cw)]
    sufs, offsets = [], []
    for h, sp in enumerate(sps):
        sufs.append([_dot(sp[:, c].astype(_BF16), tri) for c in chunks])
        tail = None
        offs = [None] * len(chunks)
        for i in reversed(range(len(chunks))):
            if runs is None:
                offs[i] = tail
            else:
                offs[i] = runs[h] if tail is None else runs[h] + tail
            rs = jnp.sum(sp[:, chunks[i]], axis=-1, keepdims=True)
            tail = rs if tail is None else tail + rs
        offsets.append(offs)
        sums.append(tail)
    ws = []
    for s, suf, offs in zip(logits, sufs, offsets):
        parts = []
        for c, sf, of in zip(chunks, suf, offs):
            a = s[:, c] - sf
            parts.append(jnp.exp2(a if of is None else a - of))
        w = parts[0] if len(parts) == 1 else jnp.concatenate(parts, axis=1)
        if mask is not None:
            w = jnp.where(mask, w, 0.0)
        ws.append(w.astype(_BF16))
    pvs = [_dot(w, v_ref[rows, c]) for w, c in zip(ws, cols)]
    return pvs, sums


def _attn_kernel(q_ref, k_ref, v_ref, tri_ref, o_ref, *, bq, bk, off, heads):
    row0 = off + pl.program_id(2) * bq
    start = pl.multiple_of(row0, bq)
    mask = lax.broadcasted_iota(jnp.int32, (bq, bq), 1) < lax.broadcasted_iota(jnp.int32, (bq, bq), 0)
    cols = [slice(h * _HEAD_DIM, (h + 1) * _HEAD_DIM) for h in range(heads)]

    def settled(runs):
        low = runs[0]
        for r in runs[1:]:
            low = jnp.minimum(low, r)
        return jnp.min(low) >= _UNDERFLOW_LOG2

    dw = min(bq, tri_ref.shape[0])
    accs, runs = _attn_blocks(q_ref, k_ref, v_ref, tri_ref[:dw, :dw], cols, pl.ds(start, bq), None, mask)
    nblk = row0 // bk

    def cond(carry):
        step, done, _, _ = carry
        return jnp.logical_and(step < nblk, jnp.logical_not(done))

    def body(carry):
        step, _, accs, runs = carry
        kstart = pl.multiple_of(row0 - (step + 1) * bk, bk)
        pvs, sums = _attn_blocks(q_ref, k_ref, v_ref, tri_ref[...], cols, pl.ds(kstart, bk), runs, None)
        runs = tuple(r + t for r, t in zip(runs, sums))
        return step + 1, settled(runs), tuple(a + p for a, p in zip(accs, pvs)), runs

    _, _, accs, _ = lax.while_loop(cond, body, (0, settled(runs), tuple(accs), tuple(runs)))
    for c, acc in zip(cols, accs):
        o_ref[:, c] = acc.astype(o_ref.dtype)


def _attention(q, k, v, off):
    b, tq, dsb = q.shape
    tk = k.shape[1]
    nh = dsb // _HEAD_DIM
    bk = _ATTN_BLOCK
    bq = min(_ATTN_BLOCK, tq)
    cw = min(_ATTN_CHUNK, bk)
    assert off % bk == 0 and (bq == bk or tq == bq) and bk % cw == 0 and (bq % cw == 0 or bq < cw)
    idx = lax.broadcasted_iota(jnp.int32, (cw, cw), 0) > lax.broadcasted_iota(jnp.int32, (cw, cw), 1)
    tri = idx.astype(_BF16)
    heads = _ATTN_HEADS
    lanes = heads * _HEAD_DIM
    return pl.pallas_call(
        functools.partial(_attn_kernel, bq=bq, bk=bk, off=off, heads=heads),
        grid=(b, nh // heads, tq // bq),
        in_specs=[
            pl.BlockSpec((None, bq, lanes), lambda i, h, j: (i, j, h)),
            pl.BlockSpec((None, tk, lanes), lambda i, h, j: (i, 0, h)),
            pl.BlockSpec((None, tk, lanes), lambda i, h, j: (i, 0, h)),
            pl.BlockSpec((cw, cw), lambda i, h, j: (0, 0)),
        ],
        out_specs=pl.BlockSpec((None, bq, lanes), lambda i, h, j: (i, j, h)),
        out_shape=jax.ShapeDtypeStruct((b, tq, dsb), _BF16),
        compiler_params=_cparams("arbitrary", "arbitrary", "arbitrary"),
        name="attention",
    )(q, k, v, tri)


def _pool_kernel(p_ref, prev_ref, hist_ref, wp_ref, sp_ref, o_ref, xcat_ref, *, bt, off, gd):
    j = pl.program_id(1)
    xcat_ref[0:_HALO, :] = jnp.where(j == 0, hist_ref[...], prev_ref[...])
    xcat_ref[_HALO:, :] = p_ref[...]
    pos = off + j * bt + lax.broadcasted_iota(jnp.int32, (bt, 1), 0)
    for g, win in enumerate(_POOL_WINDOWS):
        cols = slice(g * gd, (g + 1) * gd)
        tot = xcat_ref[_HALO:_HALO + bt, cols]
        for back in range(1, win):
            tot = tot + xcat_ref[_HALO - back:_HALO - back + bt, cols]
        cnt = jnp.minimum(win, pos + 1).astype(_F32)
        diff = (tot / cnt - p_ref[:, cols]).astype(_BF16)
        y = _dot(diff, wp_ref[g].astype(_BF16))
        o_ref[:, cols] = (y * sp_ref[:, cols]).astype(o_ref.dtype)


def _pool(p, hist, off, w_pool, s_pool, l):
    b, seq, dp = p.shape
    depth, ng, gd, _ = w_pool.shape
    bt = min(_ROW_TILE, seq)
    per = bt // _HALO
    return pl.pallas_call(
        functools.partial(_pool_kernel, bt=bt, off=off, gd=gd),
        grid=(b, seq // bt),
        in_specs=[
            pl.BlockSpec((None, bt, dp), lambda i, j: (i, j, 0)),
            pl.BlockSpec((None, _HALO, dp), lambda i, j: (i, jnp.maximum(j * per - 1, 0), 0)),
            pl.BlockSpec((None, _HALO, dp), lambda i, j: (i, 0, 0)),
            pl.BlockSpec((None, ng, gd, gd), lambda i, j: (l, 0, 0, 0)),
            pl.BlockSpec((None, 1, dp), lambda i, j: (l, 0, 0)),
        ],
        out_specs=pl.BlockSpec((None, bt, dp), lambda i, j: (i, j, 0)),
        out_shape=jax.ShapeDtypeStruct((b, seq, dp), _BF16),
        scratch_shapes=[pltpu.VMEM((bt + _HALO, dp), _F32)],
        compiler_params=_cparams("arbitrary", "arbitrary"),
        name="pool",
    )(p, p, hist, w_pool, s_pool.reshape(depth, 1, dp))


def _spatial_kernel(u_ref, vn_ref, wsp_ref, bsp_ref, o_ref, *, bt, lc, gd):
    lower = lax.broadcasted_iota(jnp.int32, (lc, lc), 0) >= lax.broadcasted_iota(jnp.int32, (lc, lc), 1)
    for g in range(_GMLP_GROUPS):
        cols = slice(g * gd, (g + 1) * gd)
        wm = jnp.where(lower, wsp_ref[g, :lc, :lc], 0.0).astype(_BF16)
        bias = bsp_ref[:lc, g:g + 1]
        for c in range(bt // lc):
            rows = slice(c * lc, (c + 1) * lc)
            sv = _dot(wm, vn_ref[rows, cols]) + bias
            o_ref[rows, cols] = (u_ref[rows, cols].astype(_F32) * sv).astype(o_ref.dtype)


def _spatial(u, vn, w_sp, b_sp_t, l):
    b, seq, dg = u.shape
    lc = _GMLP_CHUNK if seq % _GMLP_CHUNK == 0 else seq
    bt = min(_ROW_TILE, seq)
    gd = dg // _GMLP_GROUPS
    return pl.pallas_call(
        functools.partial(_spatial_kernel, bt=bt, lc=lc, gd=gd),
        grid=(b, seq // bt),
        in_specs=[
            pl.BlockSpec((None, bt, dg), lambda i, j: (i, j, 0)),
            pl.BlockSpec((None, bt, dg), lambda i, j: (i, j, 0)),
            pl.BlockSpec((None, _GMLP_GROUPS, _GMLP_CHUNK, _GMLP_CHUNK), lambda i, j: (l, 0, 0, 0)),
            pl.BlockSpec((None, _GMLP_CHUNK, _GMLP_GROUPS), lambda i, j: (l, 0, 0)),
        ],
        out_specs=pl.BlockSpec((None, bt, dg), lambda i, j: (i, j, 0)),
        out_shape=jax.ShapeDtypeStruct((b, seq, dg), _BF16),
        compiler_params=_cparams("arbitrary", "arbitrary"),
        name="spatial",
    )(u, vn, w_sp, b_sp_t)


def _trunk_layer(x, mod, k_past, v_past, hist, off, l, prm, want_gv):
    b, seq, d = x.shape
    m = b * seq
    dsb = d // 2
    dp = d // 4
    dg = d // 4
    sh1, sc1, gt1, sh2, sc2, gt2 = [mod[:, i * d:(i + 1) * d] for i in range(_N_MOD)]

    def sds(n, dt):
        return jax.ShapeDtypeStruct((m, n), dt)

    def no_extra(bm, bn):
        return []

    def head_gain(bm, bn):
        return [pl.BlockSpec((None, 1, _HEAD_DIM), lambda j, i: (l, 0, 0))]

    def two(bm, bn):
        return [_tile_spec(bm, bn), _tile_spec(bm, bn)]

    h = _norm_mod(x, prm["g_n1"], l, sc1, sh1).reshape(m, d)
    w_in = prm["w_in"]
    depth = w_in.shape[0]
    q = _mm_call(_qk_kernel, h, w_in, l, 0, dsb, [prm["g_q"].reshape(depth, 1, _HEAD_DIM)], head_gain,
                 sds(dsb, _BF16), _tile_spec, "proj_q")
    k32, k16 = _mm_call(_qk_kernel, h, w_in, l, dsb, dsb, [prm["g_k"].reshape(depth, 1, _HEAD_DIM)], head_gain,
                        [sds(dsb, _F32), sds(dsb, _BF16)], two, "proj_k")
    v32, v16 = _mm_call(_copy_kernel, h, w_in, l, 2 * dsb, dsb, [], no_extra,
                        [sds(dsb, _F32), sds(dsb, _BF16)], two, "proj_v")
    p = _mm_call(_copy_kernel, h, w_in, l, 3 * dsb, dp, [], no_extra, sds(dp, _F32), _tile_spec, "proj_p")
    u = _mm_call(_gelu_kernel, h, w_in, l, 3 * dsb + dp, dg, [], no_extra, sds(dg, _BF16), _tile_spec, "proj_u")

    def gv_gain(bm, bn):
        return [pl.BlockSpec((None, 1, dg), lambda j, i: (l, 0, 0))]

    gv_in = [prm["g_v"].reshape(depth, 1, dg)]
    if want_gv:
        gv32, vn = _mm_call(_gelu_norm_kernel, h, w_in, l, 3 * dsb + dp + dg, dg, gv_in, gv_gain,
                            [sds(dg, _F32), sds(dg, _BF16)], two, "proj_gv")
    else:
        gv32 = None
        vn = _mm_call(_gelu_norm_kernel, h, w_in, l, 3 * dsb + dp + dg, dg, gv_in, gv_gain,
                      sds(dg, _BF16), _tile_spec, "proj_gv")

    def gate_bias(bm, bn):
        return [pl.BlockSpec((None, 1, bn), lambda j, i: (l, 0, j))]

    gates = _mm_call(_gates_kernel, h, prm["w_gate"], l, 0, _N_BRANCH * d,
                     [prm["b_gate"].reshape(depth, 1, _N_BRANCH * d)], gate_bias,
                     sds(_N_BRANCH * d, _BF16), _tile_spec, "gates")

    q3 = q.reshape(b, seq, dsb)
    k3 = k16.reshape(b, seq, dsb)
    v3 = v16.reshape(b, seq, dsb)
    if k_past is not None:
        k3 = jnp.concatenate([k_past.reshape(b, off, dsb).astype(_BF16), k3], axis=1)
        v3 = jnp.concatenate([v_past.reshape(b, off, dsb).astype(_BF16), v3], axis=1)
    a = _attention(q3, k3, v3, off).reshape(m, dsb)

    p3 = p.reshape(b, seq, dp)
    pb = _pool(p3, hist, off, prm["w_pool"], prm["s_pool"], l).reshape(m, dp)
    cg = _spatial(u.reshape(b, seq, dg), vn.reshape(b, seq, dg), prm["w_sp"], prm["b_sp_t"], l).reshape(m, dg)

    merged = _merge(a, pb, cg, prm["w_br_a"], prm["w_br_b"], prm["w_br_c"], gates, l)

    def resid(gt):
        def specs(bm, bn):
            return [_tile_spec(bm, bn), pl.BlockSpec((b, bn), lambda j, i: (0, j))]
        return specs

    x1 = _mm_call(functools.partial(_out_kernel, seq=seq), merged, prm["w_out"], l, 0, d,
                  [x.reshape(m, d), gt1], resid(gt1), sds(d, _F32), _tile_spec, "out_proj")
    h2 = _norm_mod(x1.reshape(b, seq, d), prm["g_n2"], l, sc2, sh2).reshape(m, d)
    f = _mm_call(_relu2_kernel, h2, prm["w_ff1"], l, 0, prm["w_ff1"].shape[2], [], no_extra,
                 sds(prm["w_ff1"].shape[2], _BF16), _tile_spec, "ff1")
    x2 = _ff2(f, prm["w_ff2"], l, x1, gt2, seq).reshape(b, seq, d)

    nh = dsb // _HEAD_DIM
    new_k = k32.reshape(b, seq, nh, _HEAD_DIM)
    new_v = v32.reshape(b, seq, nh, _HEAD_DIM)
    new_pool = p3[:, seq - _POOL_HIST:, :]
    new_gv = gv32.reshape(b, seq, dg) if want_gv else None
    return x2, new_k, new_v, new_pool, new_gv


def kernel(x_prompt, x_sample, c_prompt, c_sample, cache_k, cache_v, state_pool, w_ada, b_ada, g_n1, w_in, g_q, g_k, w_pool, s_pool, w_sp, b_sp, g_v, w_br_a, w_br_b, w_br_c, w_gate, b_gate, w_out, g_n2, w_ff1, w_ff2):
    depth = w_in.shape[0]
    nb_p = x_prompt.shape[0]
    nb_s = x_sample.shape[0]
    past = cache_k.shape[2]
    dp = state_pool.shape[-1]
    assert nb_p + nb_s <= _C_ROWS

    prm = {
        "g_n1": g_n1, "g_q": g_q, "g_k": g_k, "g_v": g_v, "g_n2": g_n2,
        "w_pool": w_pool, "s_pool": s_pool, "w_sp": w_sp, "b_sp_t": jnp.swapaxes(b_sp, 1, 2),
        "b_gate": b_gate,
        "w_in": w_in.astype(_BF16), "w_gate": w_gate.astype(_BF16),
        "w_br_a": w_br_a.astype(_BF16), "w_br_b": w_br_b.astype(_BF16), "w_br_c": w_br_c.astype(_BF16),
        "w_out": w_out.astype(_BF16), "w_ff1": w_ff1.astype(_BF16), "w_ff2": w_ff2.astype(_BF16),
    }

    c_all = jnp.concatenate(
        [c_prompt, c_sample, jnp.zeros((_C_ROWS - nb_p - nb_s, c_prompt.shape[1]), _F32)], axis=0)
    mod = _ada(c_all, w_ada, b_ada)

    hist_p = jnp.zeros((nb_p, _HALO, dp), _F32)
    hist_s = jnp.pad(state_pool, ((0, 0), (0, 0), (_HALO - _POOL_HIST, 0), (0, 0)))

    yp, ys = x_prompt, x_sample
    outs = [[] for _ in range(7)]
    for l in range(depth):
        yp, kp, vp, pp, _ = _trunk_layer(yp, mod[l, :nb_p], None, None, hist_p, 0, l, prm, False)
        ys, ks, vs, ps, gs = _trunk_layer(ys, mod[l, nb_p:nb_p + nb_s], cache_k[l], cache_v[l], hist_s[l],
                                          past, l, prm, True)
        for lst, val in zip(outs, (kp, vp, pp, ks, vs, ps, gs)):
            lst.append(val)
    return (yp, ys) + tuple(jnp.stack(o) for o in outs)
```

```python
import functools

import jax
import jax.numpy as jnp
from jax import lax
from jax.experimental import pallas as pl
from jax.experimental.pallas import tpu as pltpu

_EPS = 1e-6
_HEAD_DIM = 128
_POOL_WINDOWS = (2, 4, 8, 16)
_POOL_HIST = max(_POOL_WINDOWS) - 1
_HALO = 16
_GMLP_GROUPS = 4
_GMLP_CHUNK = 128
_N_MOD = 6
_N_BRANCH = 3
_C_ROWS = 16

_VMEM_LIMIT_BYTES = 56 * 1024 * 1024
_BLOCK_M = 1024
_BLOCK_N = 1024
_BLOCK_K = 2048
_ATTN_BLOCK = 256
_ATTN_CHUNK = 256
_UNDERFLOW_LOG2 = 151.0
_ATTN_HEADS = 8
_LOG2E = 1.4426950408889634
_ROW_TILE = 512

_F32 = jnp.float32
_BF16 = jnp.bfloat16


def _cparams(*sem):
    return pltpu.CompilerParams(dimension_semantics=sem, vmem_limit_bytes=_VMEM_LIMIT_BYTES)


def _dot(a, b):
    return jnp.dot(a, b, preferred_element_type=_F32)


def _sigmoid(x):
    return 1.0 / (1.0 + jnp.exp(-x))


def _ada_kernel(c_ref, w_ref, b_ref, o_ref):
    c = c_ref[...]
    s = (c * _sigmoid(c)).astype(_BF16)
    o_ref[...] = _dot(s, w_ref[...].astype(_BF16)) + b_ref[...]


def _ada(c_all, w_ada, b_ada):
    depth, d, n = w_ada.shape
    bn = 512
    return pl.pallas_call(
        _ada_kernel,
        grid=(depth, n // bn),
        in_specs=[
            pl.BlockSpec((_C_ROWS, d), lambda l, j: (0, 0)),
            pl.BlockSpec((None, d, bn), lambda l, j: (l, 0, j)),
            pl.BlockSpec((None, 1, bn), lambda l, j: (l, 0, j)),
        ],
        out_specs=pl.BlockSpec((None, _C_ROWS, bn), lambda l, j: (l, 0, j)),
        out_shape=jax.ShapeDtypeStruct((depth, _C_ROWS, n), _F32),
        compiler_params=_cparams("arbitrary", "arbitrary"),
        name="ada",
    )(c_all, w_ada, b_ada.reshape(depth, 1, n))


def _norm_mod_kernel(x_ref, g_ref, sc_ref, sh_ref, o_ref):
    x = x_ref[...]
    ms = jnp.mean(x * x, axis=-1, keepdims=True)
    y = x * lax.rsqrt(ms + _EPS) * g_ref[...]
    o_ref[...] = (y * (1.0 + sc_ref[...]) + sh_ref[...]).astype(o_ref.dtype)


def _norm_mod(x, g, l, scale, shift):
    b, seq, d = x.shape
    bt = min(_ROW_TILE, seq)
    depth = g.shape[0]
    return pl.pallas_call(
        _norm_mod_kernel,
        grid=(b, seq // bt),
        in_specs=[
            pl.BlockSpec((None, bt, d), lambda i, j: (i, j, 0)),
            pl.BlockSpec((None, 1, d), lambda i, j: (l, 0, 0)),
            pl.BlockSpec((None, 1, d), lambda i, j: (i, 0, 0)),
            pl.BlockSpec((None, 1, d), lambda i, j: (i, 0, 0)),
        ],
        out_specs=pl.BlockSpec((None, bt, d), lambda i, j: (i, j, 0)),
        out_shape=jax.ShapeDtypeStruct((b, seq, d), _BF16),
        compiler_params=_cparams("arbitrary", "arbitrary"),
        name="norm_mod",
    )(x, g.reshape(depth, 1, d), scale.reshape(b, 1, d), shift.reshape(b, 1, d))


def _mm_call(kern, x, w, l, col0, ncols, extra_in, extra_specs, out_shape, out_specs, name, stacked=None):
    m, k = x.shape
    bm = min(_BLOCK_M, m)
    bn = min(_BLOCK_N, ncols)
    cb = col0 // bn
    in_specs = [
        pl.BlockSpec((bm, k), lambda j, i: (i, 0)),
        pl.BlockSpec((None, k, bn), lambda j, i: (l, 0, j + cb)),
    ] + list(extra_specs(bm, bn))
    operands = [x, w, *extra_in]
    aliases = {}
    body = kern
    if stacked is not None:
        n_in = len(operands)
        in_specs.append(pl.BlockSpec(memory_space=pl.ANY))
        operands.append(stacked)
        aliases = {n_in: 0}

        def body(*refs):
            kern(*refs[:n_in], *refs[n_in + 1:])

    return pl.pallas_call(
        body,
        grid=(ncols // bn, m // bm),
        in_specs=in_specs,
        out_specs=out_specs(bm, bn),
        out_shape=out_shape,
        input_output_aliases=aliases,
        compiler_params=_cparams("arbitrary", "arbitrary"),
        name=name,
    )(*operands)


def _tile_spec(bm, bn):
    return pl.BlockSpec((bm, bn), lambda j, i: (i, j))


def _head_norm_store(acc, g, out_refs):
    for h in range(acc.shape[1] // _HEAD_DIM):
        cols = slice(h * _HEAD_DIM, (h + 1) * _HEAD_DIM)
        a = acc[:, cols]
        ms = jnp.mean(a * a, axis=-1, keepdims=True)
        y = a * lax.rsqrt(ms + _EPS) * g
        for o_ref in out_refs:
            o_ref[:, cols] = y.astype(o_ref.dtype)


def _qk_kernel(x_ref, w_ref, g_ref, *out_refs):
    _head_norm_store(_dot(x_ref[...], w_ref[...]), g_ref[...], out_refs)


def _copy_kernel(x_ref, w_ref, *out_refs):
    acc = _dot(x_ref[...], w_ref[...])
    for o_ref in out_refs:
        o_ref[...] = acc.astype(o_ref.dtype)


def _gelu_kernel(x_ref, w_ref, o_ref):
    o_ref[...] = jax.nn.gelu(_dot(x_ref[...], w_ref[...])).astype(o_ref.dtype)


def _gelu_norm_kernel(x_ref, w_ref, g_ref, *out_refs):
    v = jax.nn.gelu(_dot(x_ref[...], w_ref[...]))
    ms = jnp.mean(v * v, axis=-1, keepdims=True)
    y = v * lax.rsqrt(ms + _EPS) * g_ref[...]
    for o_ref in out_refs:
        o_ref[...] = y.astype(o_ref.dtype)


def _gates_kernel(x_ref, w_ref, b_ref, o_ref):
    o_ref[...] = _sigmoid(_dot(x_ref[...], w_ref[...]) + b_ref[...]).astype(o_ref.dtype)


def _relu2_kernel(x_ref, w_ref, o_ref):
    r = jnp.maximum(_dot(x_ref[...], w_ref[...]), 0.0)
    o_ref[...] = (r * r).astype(o_ref.dtype)


def _gated_residual(x_ref, gt_ref, acc, o_ref, i, seq):
    bm = acc.shape[0]
    if seq >= bm:
        b = (i * bm) // seq
        o_ref[...] = x_ref[...] + gt_ref[pl.ds(b, 1), :] * acc
    else:
        per = bm // seq
        for s in range(per):
            rows = slice(s * seq, (s + 1) * seq)
            o_ref[rows, :] = x_ref[rows, :] + gt_ref[pl.ds(i * per + s, 1), :] * acc[rows, :]


def _out_kernel(m_ref, w_ref, x_ref, gt_ref, o_ref, *, seq):
    _gated_residual(x_ref, gt_ref, _dot(m_ref[...], w_ref[...]), o_ref, pl.program_id(1), seq)


def _merge_kernel(a_ref, pb_ref, cg_ref, wa_ref, wb_ref, wc_ref, ga_ref, gb_ref, gc_ref, o_ref):
    ya = _dot(a_ref[...], wa_ref[...])
    yb = _dot(pb_ref[...], wb_ref[...])
    yc = _dot(cg_ref[...], wc_ref[...])
    o = ga_ref[...].astype(_F32) * ya + gb_ref[...].astype(_F32) * yb + gc_ref[...].astype(_F32) * yc
    o_ref[...] = o.astype(o_ref.dtype)


def _merge(a, pb, cg, wa, wb, wc, gates, l):
    m = a.shape[0]
    d = wa.shape[2]
    bm = min(_BLOCK_M // 2, m)
    bn = min(_BLOCK_N, d)
    nb = d // bn

    def act(x):
        return pl.BlockSpec((bm, x.shape[1]), lambda j, i: (i, 0))

    def wt(w):
        return pl.BlockSpec((None, w.shape[1], bn), lambda j, i: (l, 0, j))

    def gate(branch):
        return pl.BlockSpec((bm, bn), lambda j, i: (i, j + branch * nb))

    return pl.pallas_call(
        _merge_kernel,
        grid=(nb, m // bm),
        in_specs=[act(a), act(pb), act(cg), wt(wa), wt(wb), wt(wc), gate(0), gate(1), gate(2)],
        out_specs=_tile_spec(bm, bn),
        out_shape=jax.ShapeDtypeStruct((m, d), _BF16),
        compiler_params=_cparams("arbitrary", "arbitrary"),
        name="merge",
    )(a, pb, cg, wa, wb, wc, gates, gates, gates)


def _ff2_kernel(f_ref, w_ref, x_ref, gt_ref, o_ref, *, seq, nk):
    k = pl.program_id(2)

    @pl.when(k == 0)
    def _():
        o_ref[...] = jnp.zeros_like(o_ref)

    o_ref[...] += _dot(f_ref[...], w_ref[...])

    @pl.when(k == nk - 1)
    def _():
        _gated_residual(x_ref, gt_ref, o_ref[...], o_ref, pl.program_id(1), seq)


def _ff2(f, w, l, x, gt, seq):
    m, kdim = f.shape
    d = w.shape[2]
    bm = min(_BLOCK_M, m)
    bn = min(_BLOCK_N, d)
    bk = min(_BLOCK_K, kdim)
    nk = kdim // bk
    nbatch = gt.shape[0]
    return pl.pallas_call(
        functools.partial(_ff2_kernel, seq=seq, nk=nk),
        grid=(d // bn, m // bm, nk),
        in_specs=[
            pl.BlockSpec((bm, bk), lambda j, i, k: (i, k)),
            pl.BlockSpec((None, bk, bn), lambda j, i, k: (l, k, j)),
            pl.BlockSpec((bm, bn), lambda j, i, k: (i, j)),
            pl.BlockSpec((nbatch, bn), lambda j, i, k: (0, j)),
        ],
        out_specs=pl.BlockSpec((bm, bn), lambda j, i, k: (i, j)),
        out_shape=jax.ShapeDtypeStruct((m, d), _F32),
        compiler_params=_cparams("arbitrary", "arbitrary", "arbitrary"),
        name="ff2",
    )(f, w, x, gt)


def _attn_blocks(q_ref, k_ref, v_ref, tri, cols, rows, runs, mask):
    scores = [lax.dot_general(q_ref[:, c], k_ref[rows, c], (((1,), (1,)), ((), ())),
                              preferred_element_type=_F32) for c in cols]
    logits, sps, sums = [], [], []
    for s in scores:
        s = s * (_HEAD_DIM ** -0.5 * _LOG2E)
        neg_abs = pltpu.bitcast(pltpu.bitcast(s, jnp.uint32) | jnp.uint32(0x80000000), _F32)
        sp = jnp.maximum(s, 0.0) + jnp.log(1.0 + jnp.exp2(neg_abs)) * _LOG2E
        logits.append(s - sp)
        if mask is not None:
            sp = jnp.where(mask, sp, 0.0)
        sps.append(sp)
    n = sps[0].shape[1]
    cw = tri.shape[0]
    chunks = [slice(c * cw, (c + 1) * cw) for c in range(n // cw)]
    sufs, offsets = [], []
    for h, sp in enumerate(sps):
        sufs.append([_dot(sp[:, c].astype(_BF16), tri) for c in chunks])
        tail = None
        offs = [None] * len(chunks)
        for i in reversed(range(len(chunks))):
            if runs is None:
                offs[i] = tail
            else:
                offs[i] = runs[h] if tail is None else runs[h] + tail
            rs = jnp.sum(sp[:, chunks[i]], axis=-1, keepdims=True)
            tail = rs if tail is None else tail + rs
        offsets.append(offs)
        sums.append(tail)
    ws = []
    for s, suf, offs in zip(logits, sufs, offsets):
        parts = []
        for c, sf, of in zip(chunks, suf, offs):
            a = s[:, c] - sf
            parts.append(jnp.exp2(a if of is None else a - of))
        w = parts[0] if len(parts) == 1 else jnp.concatenate(parts, axis=1)
        if mask is not None:
            w = jnp.where(mask, w, 0.0)
        ws.append(w.astype(_BF16))
    pvs = [_dot(w, v_ref[rows, c]) for w, c in zip(ws, cols)]
    return pvs, sums


def _attn_kernel(q_ref, k_ref, v_ref, tri_ref, o_ref, *, bq, bk, off, heads):
    row0 = off + pl.program_id(2) * bq
    start = pl.multiple_of(row0, bq)
    mask = lax.broadcasted_iota(jnp.int32, (bq, bq), 1) < lax.broadcasted_iota(jnp.int32, (bq, bq), 0)
    cols = [slice(h * _HEAD_DIM, (h + 1) * _HEAD_DIM) for h in range(heads)]

    def settled(runs):
        low = runs[0]
        for r in runs[1:]:
            low = jnp.minimum(low, r)
        return jnp.min(low) >= _UNDERFLOW_LOG2

    dw = min(bq, tri_ref.shape[0])
    accs, runs = _attn_blocks(q_ref, k_ref, v_ref, tri_ref[:dw, :dw], cols, pl.ds(start, bq), None, mask)
    nblk = row0 // bk

    def cond(carry):
        step, done, _, _ = carry
        return jnp.logical_and(step < nblk, jnp.logical_not(done))

    def body(carry):
        step, _, accs, runs = carry
        kstart = pl.multiple_of(row0 - (step + 1) * bk, bk)
        pvs, sums = _attn_blocks(q_ref, k_ref, v_ref, tri_ref[...], cols, pl.ds(kstart, bk), runs, None)
        runs = tuple(r + t for r, t in zip(runs, sums))
        return step + 1, settled(runs), tuple(a + p for a, p in zip(accs, pvs)), runs

    _, _, accs, _ = lax.while_loop(cond, body, (0, settled(runs), tuple(accs), tuple(runs)))
    for c, acc in zip(cols, accs):
        o_ref[:, c] = acc.astype(o_ref.dtype)


def _attention(q, k, v, off):
    b, tq, dsb = q.shape
    tk = k.shape[1]
    nh = dsb // _HEAD_DIM
    bk = _ATTN_BLOCK
    bq = min(_ATTN_BLOCK, tq)
    cw = min(_ATTN_CHUNK, bk)
    assert off % bk == 0 and (bq == bk or tq == bq) and bk % cw == 0 and (bq % cw == 0 or bq < cw)
    idx = lax.broadcasted_iota(jnp.int32, (cw, cw), 0) > lax.broadcasted_iota(jnp.int32, (cw, cw), 1)
    tri = idx.astype(_BF16)
    heads = _ATTN_HEADS
    lanes = heads * _HEAD_DIM
    return pl.pallas_call(
        functools.partial(_attn_kernel, bq=bq, bk=bk, off=off, heads=heads),
        grid=(b, nh // heads, tq // bq),
        in_specs=[
            pl.BlockSpec((None, bq, lanes), lambda i, h, j: (i, j, h)),
            pl.BlockSpec((None, tk, lanes), lambda i, h, j: (i, 0, h)),
            pl.BlockSpec((None, tk, lanes), lambda i, h, j: (i, 0, h)),
            pl.BlockSpec((cw, cw), lambda i, h, j: (0, 0)),
        ],
        out_specs=pl.BlockSpec((None, bq, lanes), lambda i, h, j: (i, j, h)),
        out_shape=jax.ShapeDtypeStruct((b, tq, dsb), _BF16),
        compiler_params=_cparams("arbitrary", "arbitrary", "arbitrary"),
        name="attention",
    )(q, k, v, tri)


def _pool_kernel(p_ref, prev_ref, hist_ref, wp_ref, sp_ref, o_ref, xcat_ref, *, bt, off, gd):
    j = pl.program_id(1)
    xcat_ref[0:_HALO, :] = jnp.where(j == 0, hist_ref[...], prev_ref[...])
    xcat_ref[_HALO:, :] = p_ref[...]
    pos = off + j * bt + lax.broadcasted_iota(jnp.int32, (bt, 1), 0)
    for g, win in enumerate(_POOL_WINDOWS):
        cols = slice(g * gd, (g + 1) * gd)
        tot = xcat_ref[_HALO:_HALO + bt, cols]
        for back in range(1, win):
            tot = tot + xcat_ref[_HALO - back:_HALO - back + bt, cols]
        cnt = jnp.minimum(win, pos + 1).astype(_F32)
        diff = (tot / cnt - p_ref[:, cols]).astype(_BF16)
        y = _dot(diff, wp_ref[g].astype(_BF16))
        o_ref[:, cols] = (y * sp_ref[:, cols]).astype(o_ref.dtype)


def _pool(p, hist, off, w_pool, s_pool, l):
    b, seq, dp = p.shape
    depth, ng, gd, _ = w_pool.shape
    bt = min(_ROW_TILE, seq)
    per = bt // _HALO
    return pl.pallas_call(
        functools.partial(_pool_kernel, bt=bt, off=off, gd=gd),
        grid=(b, seq // bt),
        in_specs=[
            pl.BlockSpec((None, bt, dp), lambda i, j: (i, j, 0)),
            pl.BlockSpec((None, _HALO, dp), lambda i, j: (i, jnp.maximum(j * per - 1, 0), 0)),
            pl.BlockSpec((None, _HALO, dp), lambda i, j: (i, 0, 0)),
            pl.BlockSpec((None, ng, gd, gd), lambda i, j: (l, 0, 0, 0)),
            pl.BlockSpec((None, 1, dp), lambda i, j: (l, 0, 0)),
        ],
        out_specs=pl.BlockSpec((None, bt, dp), lambda i, j: (i, j, 0)),
        out_shape=jax.ShapeDtypeStruct((b, seq, dp), _BF16),
        scratch_shapes=[pltpu.VMEM((bt + _HALO, dp), _F32)],
        compiler_params=_cparams("arbitrary", "arbitrary"),
        name="pool",
    )(p, p, hist, w_pool, s_pool.reshape(depth, 1, dp))


def _spatial_kernel(u_ref, vn_ref, wsp_ref, bsp_ref, o_ref, *, bt, lc, gd):
    lower = lax.broadcasted_iota(jnp.int32, (lc, lc), 0) >= lax.broadcasted_iota(jnp.int32, (lc, lc), 1)
    for g in range(_GMLP_GROUPS):
        cols = slice(g * gd, (g + 1) * gd)
        wm = jnp.where(lower, wsp_ref[g, :lc, :lc], 0.0).astype(_BF16)
        bias = bsp_ref[:lc, g:g + 1]
        for c in range(bt // lc):
            rows = slice(c * lc, (c + 1) * lc)
            sv = _dot(wm, vn_ref[rows, cols]) + bias
            o_ref[rows, cols] = (u_ref[rows, cols].astype(_F32) * sv).astype(o_ref.dtype)


def _spatial(u, vn, w_sp, b_sp_t, l):
    b, seq, dg = u.shape
    lc = _GMLP_CHUNK if seq % _GMLP_CHUNK == 0 else seq
    bt = min(_ROW_TILE, seq)
    gd = dg // _GMLP_GROUPS
    return pl.pallas_call(
        functools.partial(_spatial_kernel, bt=bt, lc=lc, gd=gd),
        grid=(b, seq // bt),
        in_specs=[
            pl.BlockSpec((None, bt, dg), lambda i, j: (i, j, 0)),
            pl.BlockSpec((None, bt, dg), lambda i, j: (i, j, 0)),
            pl.BlockSpec((None, _GMLP_GROUPS, _GMLP_CHUNK, _GMLP_CHUNK), lambda i, j: (l, 0, 0, 0)),
            pl.BlockSpec((None, _GMLP_CHUNK, _GMLP_GROUPS), lambda i, j: (l, 0, 0)),
        ],
        out_specs=pl.BlockSpec((None, bt, dg), lambda i, j: (i, j, 0)),
        out_shape=jax.ShapeDtypeStruct((b, seq, dg), _BF16),
        compiler_params=_cparams("arbitrary", "arbitrary"),
        name="spatial",
    )(u, vn, w_sp, b_sp_t)


def _trunk_layer(x, mod, k_past, v_past, hist, off, l, prm, want_gv, k_all, v_all):
    b, seq, d = x.shape
    m = b * seq
    dsb = d // 2
    dp = d // 4
    dg = d // 4
    sh1, sc1, gt1, sh2, sc2, gt2 = [mod[:, i * d:(i + 1) * d] for i in range(_N_MOD)]

    def sds(n, dt):
        return jax.ShapeDtypeStruct((m, n), dt)

    def no_extra(bm, bn):
        return []

    def head_gain(bm, bn):
        return [pl.BlockSpec((None, 1, _HEAD_DIM), lambda j, i: (l, 0, 0))]

    def two(bm, bn):
        return [_tile_spec(bm, bn), _tile_spec(bm, bn)]

    h = _norm_mod(x, prm["g_n1"], l, sc1, sh1).reshape(m, d)
    w_in = prm["w_in"]
    depth = w_in.shape[0]
    q = _mm_call(_qk_kernel, h, w_in, l, 0, dsb, [prm["g_q"].reshape(depth, 1, _HEAD_DIM)], head_gain,
                 sds(dsb, _BF16), _tile_spec, "proj_q")
    def layer_and_tile(bm, bn):
        return [pl.BlockSpec((None, bm, bn), lambda j, i: (l, i, j)), _tile_spec(bm, bn)]

    per_layer = jax.ShapeDtypeStruct((depth, m, dsb), _F32)
    k_all, k16 = _mm_call(_qk_kernel, h, w_in, l, dsb, dsb, [prm["g_k"].reshape(depth, 1, _HEAD_DIM)], head_gain,
                          [per_layer, sds(dsb, _BF16)], layer_and_tile, "proj_k", stacked=k_all)
    v_all, v16 = _mm_call(_copy_kernel, h, w_in, l, 2 * dsb, dsb, [], no_extra,
                          [per_layer, sds(dsb, _BF16)], layer_and_tile, "proj_v", stacked=v_all)
    p = _mm_call(_copy_kernel, h, w_in, l, 3 * dsb, dp, [], no_extra, sds(dp, _F32), _tile_spec, "proj_p")
    u = _mm_call(_gelu_kernel, h, w_in, l, 3 * dsb + dp, dg, [], no_extra, sds(dg, _BF16), _tile_spec, "proj_u")

    def gv_gain(bm, bn):
        return [pl.BlockSpec((None, 1, dg), lambda j, i: (l, 0, 0))]

    gv_in = [prm["g_v"].reshape(depth, 1, dg)]
    if want_gv:
        gv32, vn = _mm_call(_gelu_norm_kernel, h, w_in, l, 3 * dsb + dp + dg, dg, gv_in, gv_gain,
                            [sds(dg, _F32), sds(dg, _BF16)], two, "proj_gv")
    else:
        gv32 = None
        vn = _mm_call(_gelu_norm_kernel, h, w_in, l, 3 * dsb + dp + dg, dg, gv_in, gv_gain,
                      sds(dg, _BF16), _tile_spec, "proj_gv")

    def gate_bias(bm, bn):
        return [pl.BlockSpec((None, 1, bn), lambda j, i: (l, 0, j))]

    gates = _mm_call(_gates_kernel, h, prm["w_gate"], l, 0, _N_BRANCH * d,
                     [prm["b_gate"].reshape(depth, 1, _N_BRANCH * d)], gate_bias,
                     sds(_N_BRANCH * d, _BF16), _tile_spec, "gates")

    q3 = q.reshape(b, seq, dsb)
    k3 = k16.reshape(b, seq, dsb)
    v3 = v16.reshape(b, seq, dsb)
    if k_past is not None:
        k3 = jnp.concatenate([k_past.reshape(b, off, dsb).astype(_BF16), k3], axis=1)
        v3 = jnp.concatenate([v_past.reshape(b, off, dsb).astype(_BF16), v3], axis=1)
    a = _attention(q3, k3, v3, off).reshape(m, dsb)

    p3 = p.reshape(b, seq, dp)
    pb = _pool(p3, hist, off, prm["w_pool"], prm["s_pool"], l).reshape(m, dp)
    cg = _spatial(u.reshape(b, seq, dg), vn.reshape(b, seq, dg), prm["w_sp"], prm["b_sp_t"], l).reshape(m, dg)

    merged = _merge(a, pb, cg, prm["w_br_a"], prm["w_br_b"], prm["w_br_c"], gates, l)

    def resid(gt):
        def specs(bm, bn):
            return [_tile_spec(bm, bn), pl.BlockSpec((b, bn), lambda j, i: (0, j))]
        return specs

    x1 = _mm_call(functools.partial(_out_kernel, seq=seq), merged, prm["w_out"], l, 0, d,
                  [x.reshape(m, d), gt1], resid(gt1), sds(d, _F32), _tile_spec, "out_proj")
    h2 = _norm_mod(x1.reshape(b, seq, d), prm["g_n2"], l, sc2, sh2).reshape(m, d)
    f = _mm_call(_relu2_kernel, h2, prm["w_ff1"], l, 0, prm["w_ff1"].shape[2], [], no_extra,
                 sds(prm["w_ff1"].shape[2], _BF16), _tile_spec, "ff1")
    x2 = _ff2(f, prm["w_ff2"], l, x1, gt2, seq).reshape(b, seq, d)

    new_pool = p3[:, seq - _POOL_HIST:, :]
    new_gv = gv32.reshape(b, seq, dg) if want_gv else None
    return x2, k_all, v_all, new_pool, new_gv


def kernel(x_prompt, x_sample, c_prompt, c_sample, cache_k, cache_v, state_pool, w_ada, b_ada, g_n1, w_in, g_q, g_k, w_pool, s_pool, w_sp, b_sp, g_v, w_br_a, w_br_b, w_br_c, w_gate, b_gate, w_out, g_n2, w_ff1, w_ff2):
    depth = w_in.shape[0]
    nb_p = x_prompt.shape[0]
    nb_s = x_sample.shape[0]
    past = cache_k.shape[2]
    dp = state_pool.shape[-1]
    assert nb_p + nb_s <= _C_ROWS

    prm = {
        "g_n1": g_n1, "g_q": g_q, "g_k": g_k, "g_v": g_v, "g_n2": g_n2,
        "w_pool": w_pool, "s_pool": s_pool, "w_sp": w_sp, "b_sp_t": jnp.swapaxes(b_sp, 1, 2),
        "b_gate": b_gate,
        "w_in": w_in.astype(_BF16), "w_gate": w_gate.astype(_BF16),
        "w_br_a": w_br_a.astype(_BF16), "w_br_b": w_br_b.astype(_BF16), "w_br_c": w_br_c.astype(_BF16),
        "w_out": w_out.astype(_BF16), "w_ff1": w_ff1.astype(_BF16), "w_ff2": w_ff2.astype(_BF16),
    }

    c_all = jnp.concatenate(
        [c_prompt, c_sample, jnp.zeros((_C_ROWS - nb_p - nb_s, c_prompt.shape[1]), _F32)], axis=0)
    mod = _ada(c_all, w_ada, b_ada)

    hist_p = jnp.zeros((nb_p, _HALO, dp), _F32)
    hist_s = jnp.pad(state_pool, ((0, 0), (0, 0), (_HALO - _POOL_HIST, 0), (0, 0)))

    yp, ys = x_prompt, x_sample
    kp = vp = ks = vs = None
    pps, pss, gss = [], [], []
    for l in range(depth):
        yp, kp, vp, pp, _ = _trunk_layer(yp, mod[l, :nb_p], None, None, hist_p, 0, l, prm, False, kp, vp)
        ys, ks, vs, ps, gs = _trunk_layer(ys, mod[l, nb_p:nb_p + nb_s], cache_k[l], cache_v[l], hist_s[l],
                                          past, l, prm, True, ks, vs)
        pps.append(pp)
        pss.append(ps)
        gss.append(gs)
    nh = cache_k.shape[3]

    def heads(a, x):
        return a.reshape(depth, x.shape[0], x.shape[1], nh, _HEAD_DIM)

    return (yp, ys, heads(kp, x_prompt), heads(vp, x_prompt), jnp.stack(pps),
            heads(ks, x_sample), heads(vs, x_sample), jnp.stack(pss), jnp.stack(gss))
```
